```python
import math
import jax, jax.numpy as jnp
from jax import lax
import numpy as np

D_MODEL = 1024
BATCH = 4
SEQ = 8192
DEPTH = 1

HG_HEADS = 4
HG_HEAD_DIM = 128
HG_WIDTH = HG_HEADS * HG_HEAD_DIM
HG_CHUNK = 64
SG_GROUPS = 4
SG_GROUP_DIM = 128
SG_WIDTH = SG_GROUPS * SG_GROUP_DIM
SG_CHUNK = 128
FFN_MULT = 256
D_FF = -(-8 * D_MODEL // (3 * FFN_MULT)) * FFN_MULT
EPS = 1e-6
IN_SPLITS = (HG_WIDTH, HG_WIDTH, HG_WIDTH, HG_WIDTH, HG_WIDTH,
             SG_WIDTH, SG_WIDTH,
             D_MODEL, D_MODEL)
IN_COLS = sum(IN_SPLITS)

kernel_name = "hybrid_hgrn2_gmlp_gated_block"


def rmsnorm(x, w):
    xf = x.astype(jnp.float32)
    y = xf * lax.rsqrt(jnp.mean(xf * xf, axis=-1, keepdims=True) + EPS)
    return (y * w.astype(jnp.float32)).astype(x.dtype)


def layernorm(x, w, b):
    xf = x.astype(jnp.float32)
    mu = jnp.mean(xf, axis=-1, keepdims=True)
    xc = xf - mu
    y = xc * lax.rsqrt(jnp.mean(xc * xc, axis=-1, keepdims=True) + EPS)
    return (y * w.astype(jnp.float32) + b.astype(jnp.float32)).astype(x.dtype)


def hgrn2_chunk_scan(q, k, v, log_f):
    B, L, H, K = q.shape
    V = v.shape[-1]
    C = HG_CHUNK
    n = L // C
    q = q.astype(jnp.float32).reshape(B, n, C, H, K)
    k = k.astype(jnp.float32).reshape(B, n, C, H, K)
    v = v.astype(jnp.float32).reshape(B, n, C, H, V)
    b = jnp.cumsum(log_f.astype(jnp.float32).reshape(B, n, C, H, K), axis=2)
    b_last = b[:, :, -1]
    ref = b[:, :, C // 2 - 1:C // 2]
    qr = q * jnp.exp(b - ref)
    kr = k * jnp.exp(ref - b)
    scores = jnp.einsum('bnthk,bnshk->bnhts', qr, kr)
    mask = jnp.tril(jnp.ones((C, C), dtype=bool))
    scores = jnp.where(mask, scores, 0.0)
    o_intra = jnp.einsum('bnhts,bnshv->bnthv', scores, v)
    kv = jnp.einsum('bnshk,bnshv->bnhkv', k * jnp.exp(b_last[:, :, None] - b), v)
    decay = jnp.exp(b_last)

    def step(S, inp):
        kv_c, d_c = inp
        return d_c[..., None] * S + kv_c, S

    S0 = jnp.zeros((B, H, K, V), jnp.float32)
    _, S_prev = lax.scan(step, S0, (jnp.moveaxis(kv, 1, 0), jnp.moveaxis(decay, 1, 0)))
    S_prev = jnp.moveaxis(S_prev, 0, 1)
    o_inter = jnp.einsum('bnthk,bnhkv->bnthv', q * jnp.exp(b), S_prev)
    return (o_intra + o_inter).reshape(B, L, H, V)


def hgrn2_bidirectional(q, i, f_fwd_logit, f_bwd_logit, lb):
    B, L, _ = q.shape
    shp = (B, L, HG_HEADS, HG_HEAD_DIM)
    qh = q.reshape(shp)
    vh = i.reshape(shp)

    def gates(logit, lower):
        f = lower + (1.0 - lower) * jax.nn.sigmoid(logit.astype(jnp.float32))
        return (1.0 - f).reshape(shp), jnp.log(f).reshape(shp)

    k_f, lf_f = gates(f_fwd_logit, lb[0])
    k_b, lf_b = gates(f_bwd_logit, lb[1])
    o_fwd = hgrn2_chunk_scan(qh, k_f, vh, lf_f)
    flip = lambda t: jnp.flip(t, axis=1)
    o_bwd = flip(hgrn2_chunk_scan(flip(qh), flip(k_b), flip(vh), flip(lf_b)))
    return o_fwd + o_bwd


def spatial_gating(u, v, ln_w, ln_b, w_s, b_s):
    B, L, _ = u.shape
    n = L // SG_CHUNK
    u = jax.nn.gelu(u)
    v = layernorm(jax.nn.gelu(v), ln_w, ln_b)
    vc = v.reshape(B, n, SG_CHUNK, SG_GROUPS, SG_GROUP_DIM)
    mixed = jnp.einsum('gts,bnsgc->bntgc', w_s, vc) + jnp.transpose(b_s)[None, None, :, :, None]
    return u * mixed.reshape(B, L, SG_WIDTH)


def setup_inputs(seed: int = 0) -> dict:
    key = jax.random.key(seed)
    ks = jax.random.split(key, 20)
    f32 = jnp.float32
    nrm = lambda k, shp, s: jax.random.normal(k, shp, f32) * s
    gain = lambda k, shp: 1.0 + 0.02 * jax.random.normal(k, shp, f32)
    return {
        "x": jax.random.normal(ks[0], (BATCH, SEQ, D_MODEL), f32),
        "pre_mix_w": gain(ks[1], (DEPTH, D_MODEL)),
        "w_in": nrm(ks[2], (DEPTH, D_MODEL, IN_COLS), D_MODEL ** -0.5),
        "lb_logits": nrm(ks[3], (DEPTH + 1, 2, HG_WIDTH), 0.5),
        "hg_norm_w": gain(ks[4], (DEPTH, HG_WIDTH)),
        "sg_ln_w": gain(ks[5], (DEPTH, SG_WIDTH)),
        "sg_ln_b": nrm(ks[6], (DEPTH, SG_WIDTH), 0.02),
        "sg_spatial_w": nrm(ks[7], (DEPTH, SG_GROUPS, SG_CHUNK, SG_CHUNK), SG_CHUNK ** -0.5),
        "sg_spatial_b": gain(ks[8], (DEPTH, SG_GROUPS, SG_CHUNK)),
        "w_proj_a": nrm(ks[9], (DEPTH, HG_WIDTH, D_MODEL), HG_WIDTH ** -0.5),
        "w_proj_b": nrm(ks[10], (DEPTH, SG_WIDTH, D_MODEL), SG_WIDTH ** -0.5),
        "w_out": nrm(ks[11], (DEPTH, D_MODEL, D_MODEL), D_MODEL ** -0.5),
        "post_mix_w": gain(ks[12], (DEPTH, D_MODEL)),
        "pre_ffn_w": gain(ks[13], (DEPTH, D_MODEL)),
        "w_gate": nrm(ks[14], (DEPTH, D_MODEL, D_FF), D_MODEL ** -0.5),
        "w_up": nrm(ks[15], (DEPTH, D_MODEL, D_FF), D_MODEL ** -0.5),
        "w_down": nrm(ks[16], (DEPTH, D_FF, D_MODEL), D_FF ** -0.5),
        "post_ffn_w": gain(ks[17], (DEPTH, D_MODEL)),
    }


def reference(x, pre_mix_w, w_in, lb_logits, hg_norm_w, sg_ln_w, sg_ln_b, sg_spatial_w,
              sg_spatial_b, w_proj_a, w_proj_b, w_out, post_mix_w, pre_ffn_w, w_gate, w_up,
              w_down, post_ffn_w):
    B, L, _ = x.shape
    lb_all = jnp.cumsum(jax.nn.softmax(lb_logits.astype(jnp.float32), axis=0), axis=0)
    offs = [sum(IN_SPLITS[:j]) for j in range(1, len(IN_SPLITS))]
    for l in range(DEPTH):
        h = rmsnorm(x, pre_mix_w[l])
        proj = jnp.einsum('bld,dc->blc', h, w_in[l])
        q, i, f_fw, f_bw, g, u, v, ga, gb = jnp.split(proj, offs, axis=-1)
        o = hgrn2_bidirectional(q, i, f_fw, f_bw, lb_all[l])
        o = o * lax.rsqrt(jnp.mean(o * o, axis=-1, keepdims=True) + EPS)
        o = o.reshape(B, L, HG_WIDTH) * hg_norm_w[l].astype(jnp.float32)
        o = o.astype(x.dtype) * jax.nn.silu(g)
        y_a = jnp.einsum('blc,cd->bld', o, w_proj_a[l])
        s = spatial_gating(u, v, sg_ln_w[l], sg_ln_b[l], sg_spatial_w[l], sg_spatial_b[l])
        y_b = jnp.einsum('blc,cd->bld', s, w_proj_b[l])
        merged = jax.nn.sigmoid(ga) * y_a + jax.nn.sigmoid(gb) * y_b
        mix = jnp.einsum('bld,de->ble', merged, w_out[l])
        x = x + rmsnorm(mix, post_mix_w[l])
        h2 = rmsnorm(x, pre_ffn_w[l])
        ff = jax.nn.silu(jnp.einsum('bld,df->blf', h2, w_gate[l])) * jnp.einsum('bld,df->blf', h2, w_up[l])
        ff = jnp.einsum('blf,fd->bld', ff, w_down[l])
        x = x + rmsnorm(ff, post_ffn_w[l])
    return x
```

```python
import functools

import jax
import jax.numpy as jnp
from jax import lax
from jax.experimental import pallas as pl
from jax.experimental.pallas import tpu as pltpu

D_MODEL = 1024
HG_HEADS = 4
HG_HEAD_DIM = 128
HG_WIDTH = HG_HEADS * HG_HEAD_DIM
HG_CHUNK = 64
SG_GROUPS = 4
SG_GROUP_DIM = 128
SG_WIDTH = SG_GROUPS * SG_GROUP_DIM
SG_CHUNK = 128
D_FF = 2816
EPS = 1e-6

HG_COLS = 4 * HG_WIDTH
MIX_COLS = HG_WIDTH + 2 * SG_WIDTH + 2 * D_MODEL

VMEM_LIMIT_BYTES = 56 * 1024 * 1024

TM_PROJ = 512
TB_SCAN = 512
TM_MIX = 256
TM_FFN = 512
FF_TILE = 256

F32 = jnp.float32
BF16 = jnp.bfloat16

_NT = (((1,), (1,)), ((), ()))
_TN = (((0,), (0,)), ((), ()))


def _dot(a, b):
    return jnp.dot(a, b, preferred_element_type=F32)


def _rms(x, w):
    return x * lax.rsqrt(jnp.mean(x * x, axis=-1, keepdims=True) + EPS) * w


def _const_spec(shape):
    zeros = (0,) * len(shape)
    return pl.BlockSpec(shape, lambda *_: zeros, pipeline_mode=pl.Buffered(1))


def _hg_proj_kernel(x_ref, nw_ref, w_ref, q_ref, i_ref, ff_ref, fb_ref):
    h = _rms(x_ref[...], nw_ref[...]).astype(BF16)
    for n, ref in enumerate((q_ref, i_ref, ff_ref, fb_ref)):
        ref[...] = _dot(h, w_ref[:, n * HG_WIDTH:(n + 1) * HG_WIDTH])


def _hg_proj(x2, pre_w, w_hg):
    n_tok = x2.shape[0]
    out = jax.ShapeDtypeStruct((n_tok, HG_WIDTH), F32)
    tile = pl.BlockSpec((TM_PROJ, HG_WIDTH), lambda i: (i, 0))
    return pl.pallas_call(
        _hg_proj_kernel,
        grid=(n_tok // TM_PROJ,),
        in_specs=[pl.BlockSpec((TM_PROJ, D_MODEL), lambda i: (i, 0)),
                  _const_spec((1, D_MODEL)),
                  _const_spec((D_MODEL, HG_COLS))],
        out_specs=[tile] * 4,
        out_shape=[out] * 4,
        compiler_params=pltpu.CompilerParams(
            dimension_semantics=("parallel",), vmem_limit_bytes=VMEM_LIMIT_BYTES),
        name="hg_in_proj",
    )(x2, pre_w, w_hg)


def _scan_chunk(q, v, logit, lower, tri_b, tri_mask, ref_row, last_row, st_ref, o_ref, r0):
    f = lower + (1.0 - lower) * jax.nn.sigmoid(logit)
    k = 1.0 - f
    lf = jnp.log(f)
    hi = lf.astype(BF16)
    lo = (lf - hi.astype(F32)).astype(BF16)
    b = _dot(tri_b, hi) + _dot(tri_b, lo)
    b_ref = b[ref_row:ref_row + 1, :]
    b_last = b[last_row:last_row + 1, :]
    qr = (q * jnp.exp(b - b_ref)).astype(BF16)
    kr = (k * jnp.exp(b_ref - b)).astype(BF16)
    kd = (k * jnp.exp(b_last - b)).astype(BF16)
    qe = (q * jnp.exp(b)).astype(BF16)
    decay = jnp.exp(b_last)
    vb = v.astype(BF16)
    for h in range(HG_HEADS):
        sl = slice(h * HG_HEAD_DIM, (h + 1) * HG_HEAD_DIM)
        st = st_ref[h]
        scores = lax.dot_general(qr[:, sl], kr[:, sl], _NT, preferred_element_type=F32)
        scores = jnp.where(tri_mask, scores, 0.0).astype(BF16)
        o = _dot(scores, vb[:, sl])
        o = o + lax.dot_general(qe[:, sl], st.astype(BF16), _NT, preferred_element_type=F32)
        o_ref[pl.ds(r0, HG_CHUNK), sl] = o
        kv_t = lax.dot_general(vb[:, sl], kd[:, sl], _TN, preferred_element_type=F32)
        st_ref[h] = st * decay[:, sl] + kv_t


def _scan_kernel(lbl_ref, qf_ref, if_ref, ff_ref, qb_ref, ib_ref, fb_ref,
                 of_ref, ob_ref, sf_ref, sb_ref, *, layer):
    @pl.when(pl.program_id(1) == 0)
    def _():
        sf_ref[...] = jnp.zeros_like(sf_ref)
        sb_ref[...] = jnp.zeros_like(sb_ref)

    logits = lbl_ref[...]
    e = jnp.exp(logits - jnp.max(logits, axis=0, keepdims=True))
    lb = jnp.sum(e[:layer + 1], axis=0) / jnp.sum(e, axis=0)
    lb_f, lb_b = lb[0:1, :], lb[1:2, :]

    row = lax.broadcasted_iota(jnp.int32, (HG_CHUNK, HG_CHUNK), 0)
    col = lax.broadcasted_iota(jnp.int32, (HG_CHUNK, HG_CHUNK), 1)
    tril, triu = col <= row, col >= row
    tril_b = jnp.where(tril, 1.0, 0.0).astype(BF16)
    triu_b = jnp.where(triu, 1.0, 0.0).astype(BF16)
    n_chunks = TB_SCAN // HG_CHUNK
    mid = HG_CHUNK // 2

    def body(c, carry):
        r0 = pl.multiple_of(c * HG_CHUNK, HG_CHUNK)
        rows = pl.ds(r0, HG_CHUNK)
        _scan_chunk(qf_ref[rows, :], if_ref[rows, :], ff_ref[rows, :], lb_f,
                    tril_b, tril, mid - 1, HG_CHUNK - 1, sf_ref, of_ref, r0)
        r1 = pl.multiple_of((n_chunks - 1 - c) * HG_CHUNK, HG_CHUNK)
        rows = pl.ds(r1, HG_CHUNK)
        _scan_chunk(qb_ref[rows, :], ib_ref[rows, :], fb_ref[rows, :], lb_b,
                    triu_b, triu, mid, 0, sb_ref, ob_ref, r1)
        return carry

    lax.fori_loop(0, n_chunks, body, 0)


def _hg_scan(lb_logits, q, i, f_fw, f_bw, layer):
    bsz, seq, _ = q.shape
    nb = seq // TB_SCAN
    fwd = pl.BlockSpec((None, TB_SCAN, HG_WIDTH), lambda b, j: (b, j, 0))
    bwd = pl.BlockSpec((None, TB_SCAN, HG_WIDTH), lambda b, j: (b, nb - 1 - j, 0))
    out = jax.ShapeDtypeStruct((bsz, seq, HG_WIDTH), F32)
    state = pltpu.VMEM((HG_HEADS, HG_HEAD_DIM, HG_HEAD_DIM), F32)
    return pl.pallas_call(
        functools.partial(_scan_kernel, layer=layer),
        grid=(bsz, nb),
        in_specs=[_const_spec(lb_logits.shape), fwd, fwd, fwd, bwd, bwd, bwd],
        out_specs=[fwd, bwd],
        out_shape=[out, out],
        scratch_shapes=[state, state],
        compiler_params=pltpu.CompilerParams(
            dimension_semantics=("parallel", "arbitrary"), vmem_limit_bytes=VMEM_LIMIT_BYTES),
        name="hg_scan",
    )(lb_logits, q, i, f_fw, q, i, f_bw)


def _mix_kernel(x_ref, of_ref, ob_ref, prew_ref, win_ref, hgw_ref, lnw_ref, lnb_ref,
                ws_ref, bs_ref, wa_ref, wb_ref, wo_ref, postw_ref, x1_ref):
    x = x_ref[...]
    h = _rms(x, prew_ref[...]).astype(BF16)

    def proj(lo, width):
        return _dot(h, win_ref[:, lo:lo + width])

    o = of_ref[...] + ob_ref[...]
    heads = []
    for hd in range(HG_HEADS):
        oh = o[:, hd * HG_HEAD_DIM:(hd + 1) * HG_HEAD_DIM]
        heads.append(oh * lax.rsqrt(jnp.mean(oh * oh, axis=-1, keepdims=True) + EPS))
    o = jnp.concatenate(heads, axis=-1) * hgw_ref[...]
    g = proj(0, HG_WIDTH)
    y_a = _dot((o * jax.nn.silu(g)).astype(BF16), wa_ref[...])

    u = jax.nn.gelu(proj(HG_WIDTH, SG_WIDTH))
    v = jax.nn.gelu(proj(HG_WIDTH + SG_WIDTH, SG_WIDTH))
    mu = jnp.mean(v, axis=-1, keepdims=True)
    vc = v - mu
    v = vc * lax.rsqrt(jnp.mean(vc * vc, axis=-1, keepdims=True) + EPS)
    v = (v * lnw_ref[...] + lnb_ref[...]).astype(BF16)
    rows = []
    for c in range(TM_MIX // SG_CHUNK):
        r = slice(c * SG_CHUNK, (c + 1) * SG_CHUNK)
        groups = []
        for gi in range(SG_GROUPS):
            sl = slice(gi * SG_GROUP_DIM, (gi + 1) * SG_GROUP_DIM)
            groups.append(_dot(ws_ref[gi], v[r, sl]) + bs_ref[:, gi:gi + 1])
        rows.append(jnp.concatenate(groups, axis=-1))
    mixed = jnp.concatenate(rows, axis=0)
    y_b = _dot((u * mixed).astype(BF16), wb_ref[...])

    ga = proj(HG_WIDTH + 2 * SG_WIDTH, D_MODEL)
    gb = proj(HG_WIDTH + 2 * SG_WIDTH + D_MODEL, D_MODEL)
    merged = jax.nn.sigmoid(ga) * y_a + jax.nn.sigmoid(gb) * y_b
    mix = _dot(merged.astype(BF16), wo_ref[...])
    x1_ref[...] = x + _rms(mix, postw_ref[...])


def _mix(x2, o_f, o_b, pre_w, w_mix, hg_w, ln_w, ln_b, w_s, b_s_t, w_a, w_b, w_o, post_w):
    n_tok = x2.shape[0]
    return pl.pallas_call(
        _mix_kernel,
        grid=(n_tok // TM_MIX,),
        in_specs=[pl.BlockSpec((TM_MIX, D_MODEL), lambda i: (i, 0)),
                  pl.BlockSpec((TM_MIX, HG_WIDTH), lambda i: (i, 0)),
                  pl.BlockSpec((TM_MIX, HG_WIDTH), lambda i: (i, 0)),
                  _const_spec((1, D_MODEL)),
                  _const_spec((D_MODEL, MIX_COLS)),
                  _const_spec((1, HG_WIDTH)),
                  _const_spec((1, SG_WIDTH)),
                  _const_spec((1, SG_WIDTH)),
                  _const_spec((SG_GROUPS, SG_CHUNK, SG_CHUNK)),
                  _const_spec((SG_CHUNK, SG_GROUPS)),
                  _const_spec((HG_WIDTH, D_MODEL)),
                  _const_spec((SG_WIDTH, D_MODEL)),
                  _const_spec((D_MODEL, D_MODEL)),
                  _const_spec((1, D_MODEL))],
        out_specs=pl.BlockSpec((TM_MIX, D_MODEL), lambda i: (i, 0)),
        out_shape=jax.ShapeDtypeStruct((n_tok, D_MODEL), F32),
        compiler_params=pltpu.CompilerParams(
            dimension_semantics=("parallel",), vmem_limit_bytes=VMEM_LIMIT_BYTES),
        name="mixer_tail",
    )(x2, o_f, o_b, pre_w, w_mix, hg_w, ln_w, ln_b, w_s, b_s_t, w_a, w_b, w_o, post_w)


def _ffn_kernel(x_ref, prew_ref, wg_ref, wu_ref, wd_ref, postw_ref, out_ref):
    x = x_ref[...]
    h = _rms(x, prew_ref[...]).astype(BF16)
    acc = jnp.zeros((TM_FFN, D_MODEL), F32)
    for t in range(D_FF // FF_TILE):
        cols = slice(t * FF_TILE, (t + 1) * FF_TILE)
        act = jax.nn.silu(_dot(h, wg_ref[:, cols])) * _dot(h, wu_ref[:, cols])
        acc = acc + _dot(act.astype(BF16), wd_ref[cols, :])
    out_ref[...] = x + _rms(acc, postw_ref[...])


def _ffn(x1, pre_w, w_g, w_u, w_d, post_w):
    n_tok = x1.shape[0]
    tile = pl.BlockSpec((TM_FFN, D_MODEL), lambda i: (i, 0))
    return pl.pallas_call(
        _ffn_kernel,
        grid=(n_tok // TM_FFN,),
        in_specs=[tile,
                  _const_spec((1, D_MODEL)),
                  _const_spec((D_MODEL, D_FF)),
                  _const_spec((D_MODEL, D_FF)),
                  _const_spec((D_FF, D_MODEL)),
                  _const_spec((1, D_MODEL))],
        out_specs=tile,
        out_shape=jax.ShapeDtypeStruct((n_tok, D_MODEL), F32),
        compiler_params=pltpu.CompilerParams(
            dimension_semantics=("parallel",), vmem_limit_bytes=VMEM_LIMIT_BYTES),
        name="swiglu_ffn",
    )(x1, pre_w, w_g, w_u, w_d, post_w)


def kernel(x, pre_mix_w, w_in, lb_logits, hg_norm_w, sg_ln_w, sg_ln_b, sg_spatial_w,
           sg_spatial_b, w_proj_a, w_proj_b, w_out, post_mix_w, pre_ffn_w, w_gate, w_up,
           w_down, post_ffn_w):
    bsz, seq, d = x.shape
    depth = pre_mix_w.shape[0]
    assert d == D_MODEL and seq % TB_SCAN == 0 and (bsz * seq) % TM_FFN == 0
    x2 = x.reshape(bsz * seq, d)
    lb_logits = lb_logits.astype(F32)
    for l in range(depth):
        w_in_b = w_in[l].astype(BF16)
        q, i, f_fw, f_bw = _hg_proj(x2, pre_mix_w[l][None], w_in_b[:, :HG_COLS])
        shp = (bsz, seq, HG_WIDTH)
        o_f, o_b = _hg_scan(lb_logits, q.reshape(shp), i.reshape(shp),
                            f_fw.reshape(shp), f_bw.reshape(shp), l)
        x1 = _mix(x2, o_f.reshape(-1, HG_WIDTH), o_b.reshape(-1, HG_WIDTH),
                  pre_mix_w[l][None], w_in_b[:, HG_COLS:], hg_norm_w[l][None],
                  sg_ln_w[l][None], sg_ln_b[l][None], sg_spatial_w[l].astype(BF16),
                  sg_spatial_b[l].T, w_proj_a[l].astype(BF16), w_proj_b[l].astype(BF16),
                  w_out[l].astype(BF16), post_mix_w[l][None])
        x2 = _ffn(x1, pre_ffn_w[l][None], w_gate[l].astype(BF16), w_up[l].astype(BF16),
                  w_down[l].astype(BF16), post_ffn_w[l][None])
    return x2.reshape(bsz, seq, d)
```

```python
import functools

import jax
import jax.numpy as jnp
from jax import lax
from jax.experimental import pallas as pl
from jax.experimental.pallas import tpu as pltpu

D_MODEL = 1024
HG_HEADS = 4
HG_HEAD_DIM = 128
HG_WIDTH = HG_HEADS * HG_HEAD_DIM
HG_CHUNK = 64
SG_GROUPS = 4
SG_GROUP_DIM = 128
SG_WIDTH = SG_GROUPS * SG_GROUP_DIM
SG_CHUNK = 128
D_FF = 2816
EPS = 1e-6

HG_COLS = 4 * HG_WIDTH
MIX_COLS = HG_WIDTH + 2 * SG_WIDTH + 2 * D_MODEL

VMEM_LIMIT_BYTES = 56 * 1024 * 1024

TM_PROJ = 512
TB_SCAN = 512
TM_MIX = 256
TM_FFN = 512
FF_TILE = 256

F32 = jnp.float32
BF16 = jnp.bfloat16

_NT = (((1,), (1,)), ((), ()))
_TN = (((0,), (0,)), ((), ()))


def _dot(a, b):
    return jnp.dot(a, b, preferred_element_type=F32)


def _rms(x, w):
    return x * lax.rsqrt(jnp.mean(x * x, axis=-1, keepdims=True) + EPS) * w


def _const_spec(shape):
    zeros = (0,) * len(shape)
    return pl.BlockSpec(shape, lambda *_: zeros, pipeline_mode=pl.Buffered(1))


def _hg_proj_kernel(x_ref, nw_ref, w_ref, q_ref, i_ref, ff_ref, fb_ref):
    h = _rms(x_ref[...], nw_ref[...]).astype(BF16)
    for n, ref in enumerate((q_ref, i_ref, ff_ref, fb_ref)):
        ref[...] = _dot(h, w_ref[:, n * HG_WIDTH:(n + 1) * HG_WIDTH])


def _hg_proj(x2, pre_w, w_hg):
    n_tok = x2.shape[0]
    out = jax.ShapeDtypeStruct((n_tok, HG_WIDTH), F32)
    tile = pl.BlockSpec((TM_PROJ, HG_WIDTH), lambda i: (i, 0))
    return pl.pallas_call(
        _hg_proj_kernel,
        grid=(n_tok // TM_PROJ,),
        in_specs=[pl.BlockSpec((TM_PROJ, D_MODEL), lambda i: (i, 0)),
                  _const_spec((1, D_MODEL)),
                  _const_spec((D_MODEL, HG_COLS))],
        out_specs=[tile] * 4,
        out_shape=[out] * 4,
        compiler_params=pltpu.CompilerParams(
            dimension_semantics=("parallel",), vmem_limit_bytes=VMEM_LIMIT_BYTES),
        name="hg_in_proj",
    )(x2, pre_w, w_hg)


def _scan_kernel(lbl_ref, qf_ref, if_ref, ff_ref, qb_ref, ib_ref, fb_ref, of_ref, ob_ref,
                 st_ref, k_ref, hi_ref, lo_ref, b_ref, qr_ref, kr_ref, kd_ref, qe_ref,
                 sc_ref, kvt_ref, dec_ref, *, layer):
    @pl.when(pl.program_id(1) == 0)
    def _():
        st_ref[...] = jnp.zeros_like(st_ref)

    logits = lbl_ref[...]
    e = jnp.exp(logits - jnp.max(logits, axis=0, keepdims=True))
    lb = jnp.sum(e[:layer + 1], axis=0) / jnp.sum(e, axis=0)

    row = lax.broadcasted_iota(jnp.int32, (HG_CHUNK, HG_CHUNK), 0)
    col = lax.broadcasted_iota(jnp.int32, (HG_CHUNK, HG_CHUNK), 1)
    n_chunks = TB_SCAN // HG_CHUNK
    mid = HG_CHUNK // 2
    q_refs, v_refs, f_refs, o_refs = (qf_ref, qb_ref), (if_ref, ib_ref), (ff_ref, fb_ref), (of_ref, ob_ref)
    masks = (col <= row, col >= row)
    tris = tuple(jnp.where(m, 1.0, 0.0).astype(BF16) for m in masks)
    ref_rows, last_rows = (mid - 1, mid), (HG_CHUNK - 1, 0)
    units = [(d, c) for c in range(n_chunks) for d in range(2)]
    heads = [slice(h * HG_HEAD_DIM, (h + 1) * HG_HEAD_DIM) for h in range(HG_HEADS)]

    def rows(c):
        return pl.ds(c * HG_CHUNK, HG_CHUNK)

    for d, c in units:
        lower = lb[d:d + 1, :]
        f = lower + (1.0 - lower) * jax.nn.sigmoid(f_refs[d][rows(c), :])
        k_ref[d, c] = 1.0 - f
        lf = jnp.log(f)
        hi = lf.astype(BF16)
        hi_ref[d, c] = hi
        lo_ref[d, c] = (lf - hi.astype(F32)).astype(BF16)
    for d, c in units:
        b_ref[d, c] = _dot(tris[d], hi_ref[d, c]) + _dot(tris[d], lo_ref[d, c])
    for d, c in units:
        b = b_ref[d, c]
        b_mid = b[ref_rows[d]:ref_rows[d] + 1, :]
        b_last = b[last_rows[d]:last_rows[d] + 1, :]
        qr = q_refs[d][rows(c), :] * jnp.exp(b - b_mid)
        kr = k_ref[d, c] * jnp.exp(b_mid - b)
        qr_ref[d, c] = qr.astype(BF16)
        kr_ref[d, c] = kr.astype(BF16)
        qe_ref[d, c] = (qr * jnp.exp(b_mid)).astype(BF16)
        kd_ref[d, c] = (kr * jnp.exp(b_last - b_mid)).astype(BF16)
        dec_ref[d, c] = jnp.exp(b_last)
    for d, c in units:
        for h, sl in enumerate(heads):
            s = lax.dot_general(qr_ref[d, c, :, sl], kr_ref[d, c, :, sl], _NT,
                                preferred_element_type=F32)
            sc_ref[d, c, h] = jnp.where(masks[d], s, 0.0).astype(BF16)
    for d, c in units:
        vb = v_refs[d][rows(c), :].astype(BF16)
        for h, sl in enumerate(heads):
            o_refs[d][rows(c), sl] = _dot(sc_ref[d, c, h], vb[:, sl])
            kvt_ref[d, c, h] = lax.dot_general(vb[:, sl], kd_ref[d, c, :, sl], _TN,
                                               preferred_element_type=F32)
    for step in range(n_chunks):
        for d in range(2):
            c = step if d == 0 else n_chunks - 1 - step
            decay = dec_ref[d, c]
            for h, sl in enumerate(heads):
                st = st_ref[d, h]
                o_refs[d][rows(c), sl] += lax.dot_general(
                    qe_ref[d, c, :, sl], st.astype(BF16), _NT, preferred_element_type=F32)
                st_ref[d, h] = st * decay[:, sl] + kvt_ref[d, c, h]


def _hg_scan(lb_logits, q, i, f_fw, f_bw, layer):
    bsz, seq, _ = q.shape
    nb = seq // TB_SCAN
    n_chunks = TB_SCAN // HG_CHUNK
    fwd = pl.BlockSpec((None, TB_SCAN, HG_WIDTH), lambda b, j: (b, j, 0))
    bwd = pl.BlockSpec((None, TB_SCAN, HG_WIDTH), lambda b, j: (b, nb - 1 - j, 0))
    out = jax.ShapeDtypeStruct((bsz, seq, HG_WIDTH), F32)
    stage = (2, n_chunks, HG_CHUNK, HG_WIDTH)
    return pl.pallas_call(
        functools.partial(_scan_kernel, layer=layer),
        grid=(bsz, nb),
        in_specs=[_const_spec(lb_logits.shape), fwd, fwd, fwd, bwd, bwd, bwd],
        out_specs=[fwd, bwd],
        out_shape=[out, out],
        scratch_shapes=[
            pltpu.VMEM((2, HG_HEADS, HG_HEAD_DIM, HG_HEAD_DIM), F32),
            pltpu.VMEM(stage, F32),
            pltpu.VMEM(stage, BF16), pltpu.VMEM(stage, BF16),
            pltpu.VMEM(stage, F32),
            pltpu.VMEM(stage, BF16), pltpu.VMEM(stage, BF16),
            pltpu.VMEM(stage, BF16), pltpu.VMEM(stage, BF16),
            pltpu.VMEM((2, n_chunks, HG_HEADS, HG_CHUNK, HG_CHUNK), BF16),
            pltpu.VMEM((2, n_chunks, HG_HEADS, HG_HEAD_DIM, HG_HEAD_DIM), F32),
            pltpu.VMEM((2, n_chunks, 1, HG_WIDTH), F32),
        ],
        compiler_params=pltpu.CompilerParams(
            dimension_semantics=("parallel", "arbitrary"), vmem_limit_bytes=VMEM_LIMIT_BYTES),
        name="hg_scan",
    )(lb_logits, q, i, f_fw, q, i, f_bw)


def _mix_kernel(x_ref, of_ref, ob_ref, prew_ref, win_ref, hgw_ref, lnw_ref, lnb_ref,
                ws_ref, bs_ref, wa_ref, wb_ref, wo_ref, postw_ref, x1_ref):
    x = x_ref[...]
    h = _rms(x, prew_ref[...]).astype(BF16)

    def proj(lo, width):
        return _dot(h, win_ref[:, lo:lo + width])

    o = of_ref[...] + ob_ref[...]
    heads = []
    for hd in range(HG_HEADS):
        oh = o[:, hd * HG_HEAD_DIM:(hd + 1) * HG_HEAD_DIM]
        heads.append(oh * lax.rsqrt(jnp.mean(oh * oh, axis=-1, keepdims=True) + EPS))
    o = jnp.concatenate(heads, axis=-1) * hgw_ref[...]
    g = proj(0, HG_WIDTH)
    y_a = _dot((o * jax.nn.silu(g)).astype(BF16), wa_ref[...])

    u = jax.nn.gelu(proj(HG_WIDTH, SG_WIDTH))
    v = jax.nn.gelu(proj(HG_WIDTH + SG_WIDTH, SG_WIDTH))
    mu = jnp.mean(v, axis=-1, keepdims=True)
    vc = v - mu
    v = vc * lax.rsqrt(jnp.mean(vc * vc, axis=-1, keepdims=True) + EPS)
    v = (v * lnw_ref[...] + lnb_ref[...]).astype(BF16)
    rows = []
    for c in range(TM_MIX // SG_CHUNK):
        r = slice(c * SG_CHUNK, (c + 1) * SG_CHUNK)
        groups = []
        for gi in range(SG_GROUPS):
            sl = slice(gi * SG_GROUP_DIM, (gi + 1) * SG_GROUP_DIM)
            groups.append(_dot(ws_ref[gi], v[r, sl]) + bs_ref[:, gi:gi + 1])
        rows.append(jnp.concatenate(groups, axis=-1))
    mixed = jnp.concatenate(rows, axis=0)
    y_b = _dot((u * mixed).astype(BF16), wb_ref[...])

    ga = proj(HG_WIDTH + 2 * SG_WIDTH, D_MODEL)
    gb = proj(HG_WIDTH + 2 * SG_WIDTH + D_MODEL, D_MODEL)
    merged = jax.nn.sigmoid(ga) * y_a + jax.nn.sigmoid(gb) * y_b
    mix = _dot(merged.astype(BF16), wo_ref[...])
    x1_ref[...] = x + _rms(mix, postw_ref[...])


def _mix(x2, o_f, o_b, pre_w, w_mix, hg_w, ln_w, ln_b, w_s, b_s_t, w_a, w_b, w_o, post_w):
    n_tok = x2.shape[0]
    return pl.pallas_call(
        _mix_kernel,
        grid=(n_tok // TM_MIX,),
        in_specs=[pl.BlockSpec((TM_MIX, D_MODEL), lambda i: (i, 0)),
                  pl.BlockSpec((TM_MIX, HG_WIDTH), lambda i: (i, 0)),
                  pl.BlockSpec((TM_MIX, HG_WIDTH), lambda i: (i, 0)),
                  _const_spec((1, D_MODEL)),
                  _const_spec((D_MODEL, MIX_COLS)),
                  _const_spec((1, HG_WIDTH)),
                  _const_spec((1, SG_WIDTH)),
                  _const_spec((1, SG_WIDTH)),
                  _const_spec((SG_GROUPS, SG_CHUNK, SG_CHUNK)),
                  _const_spec((SG_CHUNK, SG_GROUPS)),
                  _const_spec((HG_WIDTH, D_MODEL)),
                  _const_spec((SG_WIDTH, D_MODEL)),
                  _const_spec((D_MODEL, D_MODEL)),
                  _const_spec((1, D_MODEL))],
        out_specs=pl.BlockSpec((TM_MIX, D_MODEL), lambda i: (i, 0)),
        out_shape=jax.ShapeDtypeStruct((n_tok, D_MODEL), F32),
        compiler_params=pltpu.CompilerParams(
            dimension_semantics=("parallel",), vmem_limit_bytes=VMEM_LIMIT_BYTES),
        name="mixer_tail",
    )(x2, o_f, o_b, pre_w, w_mix, hg_w, ln_w, ln_b, w_s, b_s_t, w_a, w_b, w_o, post_w)


def _ffn_kernel(x_ref, prew_ref, wg_ref, wu_ref, wd_ref, postw_ref, out_ref):
    x = x_ref[...]
    h = _rms(x, prew_ref[...]).astype(BF16)
    acc = jnp.zeros((TM_FFN, D_MODEL), F32)
    for t in range(D_FF // FF_TILE):
        cols = slice(t * FF_TILE, (t + 1) * FF_TILE)
        act = jax.nn.silu(_dot(h, wg_ref[:, cols])) * _dot(h, wu_ref[:, cols])
        acc = acc + _dot(act.astype(BF16), wd_ref[cols, :])
    out_ref[...] = x + _rms(acc, postw_ref[...])


def _ffn(x1, pre_w, w_g, w_u, w_d, post_w):
    n_tok = x1.shape[0]
    tile = pl.BlockSpec((TM_FFN, D_MODEL), lambda i: (i, 0))
    return pl.pallas_call(
        _ffn_kernel,
        grid=(n_tok // TM_FFN,),
        in_specs=[tile,
                  _const_spec((1, D_MODEL)),
                  _const_spec((D_MODEL, D_FF)),
                  _const_spec((D_MODEL, D_FF)),
                  _const_spec((D_FF, D_MODEL)),
                  _const_spec((1, D_MODEL))],
        out_specs=tile,
        out_shape=jax.ShapeDtypeStruct((n_tok, D_MODEL), F32),
        compiler_params=pltpu.CompilerParams(
            dimension_semantics=("parallel",), vmem_limit_bytes=VMEM_LIMIT_BYTES),
        name="swiglu_ffn",
    )(x1, pre_w, w_g, w_u, w_d, post_w)


def kernel(x, pre_mix_w, w_in, lb_logits, hg_norm_w, sg_ln_w, sg_ln_b, sg_spatial_w,
           sg_spatial_b, w_proj_a, w_proj_b, w_out, post_mix_w, pre_ffn_w, w_gate, w_up,
           w_down, post_ffn_w):
    bsz, seq, d = x.shape
    depth = pre_mix_w.shape[0]
    assert d == D_MODEL and seq % TB_SCAN == 0 and (bsz * seq) % TM_FFN == 0
    x2 = x.reshape(bsz * seq, d)
    lb_logits = lb_logits.astype(F32)
    for l in range(depth):
        w_in_b = w_in[l].astype(BF16)
        q, i, f_fw, f_bw = _hg_proj(x2, pre_mix_w[l][None], w_in_b[:, :HG_COLS])
        shp = (bsz, seq, HG_WIDTH)
        o_f, o_b = _hg_scan(lb_logits, q.reshape(shp), i.reshape(shp),
                            f_fw.reshape(shp), f_bw.reshape(shp), l)
        x1 = _mix(x2, o_f.reshape(-1, HG_WIDTH), o_b.reshape(-1, HG_WIDTH),
                  pre_mix_w[l][None], w_in_b[:, HG_COLS:], hg_norm_w[l][None],
                  sg_ln_w[l][None], sg_ln_b[l][None], sg_spatial_w[l].astype(BF16),
                  sg_spatial_b[l].T, w_proj_a[l].astype(BF16), w_proj_b[l].astype(BF16),
                  w_out[l].astype(BF16), post_mix_w[l][None])
        x2 = _ffn(x1, pre_ffn_w[l][None], w_gate[l].astype(BF16), w_up[l].astype(BF16),
                  w_down[l].astype(BF16), post_ffn_w[l][None])
    return x2.reshape(bsz, seq, d)
```

```python
import functools

import jax
import jax.numpy as jnp
from jax import lax
from jax.experimental import pallas as pl
from jax.experimental.pallas import tpu as pltpu

D_MODEL = 1024
HG_HEADS = 4
HG_HEAD_DIM = 128
HG_WIDTH = HG_HEADS * HG_HEAD_DIM
HG_CHUNK = 64
SG_GROUPS = 4
SG_GROUP_DIM = 128
SG_WIDTH = SG_GROUPS * SG_GROUP_DIM
SG_CHUNK = 128
D_FF = 2816
EPS = 1e-6

HG_COLS = 4 * HG_WIDTH
MIX_COLS = HG_WIDTH + 2 * SG_WIDTH + 2 * D_MODEL

VMEM_LIMIT_BYTES = 56 * 1024 * 1024

TM_PROJ, SUB_PROJ = 1024, 512
TB_SCAN = 512
TM_MIX, SUB_MIX = 1024, 256
TM_FFN, SUB_FFN = 1024, 512
FF_TILE = 256

F32 = jnp.float32
BF16 = jnp.bfloat16

_NT = (((1,), (1,)), ((), ()))
_TN = (((0,), (0,)), ((), ()))


def _dot(a, b):
    return jnp.dot(a, b, preferred_element_type=F32)


def _rms(x, w):
    return x * lax.rsqrt(jnp.mean(x * x, axis=-1, keepdims=True) + EPS) * w


def _run_skewed(stage_iters):
    pending, live = list(stage_iters), []
    while pending or live:
        if pending:
            live.append(pending.pop(0))
        for it in list(live):
            if next(it, StopIteration) is StopIteration:
                live.remove(it)


def _const_spec(shape):
    zeros = (0,) * len(shape)
    return pl.BlockSpec(shape, lambda *_: zeros, pipeline_mode=pl.Buffered(1))


def _hg_proj_kernel(x_ref, nw_ref, w_ref, q_ref, i_ref, ff_ref, fb_ref):
    def stages(rows):
        h = _rms(x_ref[rows, :], nw_ref[...]).astype(BF16)
        yield
        for n, ref in enumerate((q_ref, i_ref, ff_ref, fb_ref)):
            ref[rows, :] = _dot(h, w_ref[:, n * HG_WIDTH:(n + 1) * HG_WIDTH])
            yield

    _run_skewed(stages(pl.ds(s * SUB_PROJ, SUB_PROJ)) for s in range(TM_PROJ // SUB_PROJ))


def _hg_proj(x2, pre_w, w_hg):
    n_tok = x2.shape[0]
    out = jax.ShapeDtypeStruct((n_tok, HG_WIDTH), F32)
    tile = pl.BlockSpec((TM_PROJ, HG_WIDTH), lambda i: (i, 0))
    return pl.pallas_call(
        _hg_proj_kernel,
        grid=(n_tok // TM_PROJ,),
        in_specs=[pl.BlockSpec((TM_PROJ, D_MODEL), lambda i: (i, 0)),
                  _const_spec((1, D_MODEL)),
                  _const_spec((D_MODEL, HG_COLS))],
        out_specs=[tile] * 4,
        out_shape=[out] * 4,
        compiler_params=pltpu.CompilerParams(
            dimension_semantics=("parallel",), vmem_limit_bytes=VMEM_LIMIT_BYTES),
        name="hg_in_proj",
    )(x2, pre_w, w_hg)


def _scan_kernel(lbl_ref, qf_ref, if_ref, ff_ref, qb_ref, ib_ref, fb_ref, of_ref, ob_ref,
                 st_ref, k_ref, hi_ref, lo_ref, b_ref, qr_ref, kr_ref, kd_ref, qe_ref,
                 sc_ref, kvt_ref, dec_ref, *, layer):
    @pl.when(pl.program_id(1) == 0)
    def _():
        st_ref[...] = jnp.zeros_like(st_ref)

    logits = lbl_ref[...]
    e = jnp.exp(logits - jnp.max(logits, axis=0, keepdims=True))
    lb = jnp.sum(e[:layer + 1], axis=0) / jnp.sum(e, axis=0)

    row = lax.broadcasted_iota(jnp.int32, (HG_CHUNK, HG_CHUNK), 0)
    col = lax.broadcasted_iota(jnp.int32, (HG_CHUNK, HG_CHUNK), 1)
    n_chunks = TB_SCAN // HG_CHUNK
    mid = HG_CHUNK // 2
    q_refs, v_refs, f_refs, o_refs = (qf_ref, qb_ref), (if_ref, ib_ref), (ff_ref, fb_ref), (of_ref, ob_ref)
    masks = (col <= row, col >= row)
    tris = tuple(jnp.where(m, 1.0, 0.0).astype(BF16) for m in masks)
    ref_rows, last_rows = (mid - 1, mid), (HG_CHUNK - 1, 0)
    units = [(d, c) for c in range(n_chunks) for d in range(2)]
    heads = [slice(h * HG_HEAD_DIM, (h + 1) * HG_HEAD_DIM) for h in range(HG_HEADS)]

    def rows(c):
        return pl.ds(c * HG_CHUNK, HG_CHUNK)

    for d, c in units:
        lower = lb[d:d + 1, :]
        f = lower + (1.0 - lower) * jax.nn.sigmoid(f_refs[d][rows(c), :])
        k_ref[d, c] = 1.0 - f
        lf = jnp.log(f)
        hi = lf.astype(BF16)
        hi_ref[d, c] = hi
        lo_ref[d, c] = (lf - hi.astype(F32)).astype(BF16)
    for d, c in units:
        b_ref[d, c] = _dot(tris[d], hi_ref[d, c]) + _dot(tris[d], lo_ref[d, c])
    for d, c in units:
        b = b_ref[d, c]
        b_mid = b[ref_rows[d]:ref_rows[d] + 1, :]
        b_last = b[last_rows[d]:last_rows[d] + 1, :]
        qr = q_refs[d][rows(c), :] * jnp.exp(b - b_mid)
        kr = k_ref[d, c] * jnp.exp(b_mid - b)
        qr_ref[d, c] = qr.astype(BF16)
        kr_ref[d, c] = kr.astype(BF16)
        qe_ref[d, c] = (qr * jnp.exp(b_mid)).astype(BF16)
        kd_ref[d, c] = (kr * jnp.exp(b_last - b_mid)).astype(BF16)
        dec_ref[d, c] = jnp.exp(b_last)
    for d, c in units:
        for h, sl in enumerate(heads):
            s = lax.dot_general(qr_ref[d, c, :, sl], kr_ref[d, c, :, sl], _NT,
                                preferred_element_type=F32)
            sc_ref[d, c, h] = jnp.where(masks[d], s, 0.0).astype(BF16)
    for d, c in units:
        vb = v_refs[d][rows(c), :].astype(BF16)
        for h, sl in enumerate(heads):
            o_refs[d][rows(c), sl] = _dot(sc_ref[d, c, h], vb[:, sl])
            kvt_ref[d, c, h] = lax.dot_general(vb[:, sl], kd_ref[d, c, :, sl], _TN,
                                               preferred_element_type=F32)
    for step in range(n_chunks):
        for d in range(2):
            c = step if d == 0 else n_chunks - 1 - step
            decay = dec_ref[d, c]
            for h, sl in enumerate(heads):
                st = st_ref[d, h]
                o_refs[d][rows(c), sl] += lax.dot_general(
                    qe_ref[d, c, :, sl], st.astype(BF16), _NT, preferred_element_type=F32)
                st_ref[d, h] = st * decay[:, sl] + kvt_ref[d, c, h]


def _hg_scan(lb_logits, q, i, f_fw, f_bw, layer):
    bsz, seq, _ = q.shape
    nb = seq // TB_SCAN
    n_chunks = TB_SCAN // HG_CHUNK
    fwd = pl.BlockSpec((None, TB_SCAN, HG_WIDTH), lambda b, j: (b, j, 0))
    bwd = pl.BlockSpec((None, TB_SCAN, HG_WIDTH), lambda b, j: (b, nb - 1 - j, 0))
    out = jax.ShapeDtypeStruct((bsz, seq, HG_WIDTH), F32)
    stage = (2, n_chunks, HG_CHUNK, HG_WIDTH)
    return pl.pallas_call(
        functools.partial(_scan_kernel, layer=layer),
        grid=(bsz, nb),
        in_specs=[_const_spec(lb_logits.shape), fwd, fwd, fwd, bwd, bwd, bwd],
        out_specs=[fwd, bwd],
        out_shape=[out, out],
        scratch_shapes=[
            pltpu.VMEM((2, HG_HEADS, HG_HEAD_DIM, HG_HEAD_DIM), F32),
            pltpu.VMEM(stage, F32),
            pltpu.VMEM(stage, BF16), pltpu.VMEM(stage, BF16),
            pltpu.VMEM(stage, F32),
            pltpu.VMEM(stage, BF16), pltpu.VMEM(stage, BF16),
            pltpu.VMEM(stage, BF16), pltpu.VMEM(stage, BF16),
            pltpu.VMEM((2, n_chunks, HG_HEADS, HG_CHUNK, HG_CHUNK), BF16),
            pltpu.VMEM((2, n_chunks, HG_HEADS, HG_HEAD_DIM, HG_HEAD_DIM), F32),
            pltpu.VMEM((2, n_chunks, 1, HG_WIDTH), F32),
        ],
        compiler_params=pltpu.CompilerParams(
            dimension_semantics=("parallel", "arbitrary"), vmem_limit_bytes=VMEM_LIMIT_BYTES),
        name="hg_scan",
    )(lb_logits, q, i, f_fw, q, i, f_bw)


def _mix_stages(rows, x_ref, of_ref, ob_ref, prew_ref, win_ref, hgw_ref, lnw_ref, lnb_ref,
                ws_ref, bs_ref, wa_ref, wb_ref, wo_ref, postw_ref, x1_ref):
    x = x_ref[rows, :]
    h = _rms(x, prew_ref[...]).astype(BF16)

    def proj(lo, width):
        return _dot(h, win_ref[:, lo:lo + width])
    yield

    g = proj(0, HG_WIDTH)
    u = proj(HG_WIDTH, SG_WIDTH)
    v = proj(HG_WIDTH + SG_WIDTH, SG_WIDTH)
    yield

    o = of_ref[rows, :] + ob_ref[rows, :]
    heads = []
    for hd in range(HG_HEADS):
        oh = o[:, hd * HG_HEAD_DIM:(hd + 1) * HG_HEAD_DIM]
        heads.append(oh * lax.rsqrt(jnp.mean(oh * oh, axis=-1, keepdims=True) + EPS))
    o = jnp.concatenate(heads, axis=-1) * hgw_ref[...]
    a_in = (o * jax.nn.silu(g)).astype(BF16)
    u = jax.nn.gelu(u)
    v = jax.nn.gelu(v)
    mu = jnp.mean(v, axis=-1, keepdims=True)
    vc = v - mu
    v = vc * lax.rsqrt(jnp.mean(vc * vc, axis=-1, keepdims=True) + EPS)
    v = (v * lnw_ref[...] + lnb_ref[...]).astype(BF16)
    yield

    ga = proj(HG_WIDTH + 2 * SG_WIDTH, D_MODEL)
    y_a = _dot(a_in, wa_ref[...])
    chunks = []
    for c in range(SUB_MIX // SG_CHUNK):
        r = slice(c * SG_CHUNK, (c + 1) * SG_CHUNK)
        groups = []
        for gi in range(SG_GROUPS):
            sl = slice(gi * SG_GROUP_DIM, (gi + 1) * SG_GROUP_DIM)
            groups.append(_dot(ws_ref[gi], v[r, sl]) + bs_ref[:, gi:gi + 1])
        chunks.append(jnp.concatenate(groups, axis=-1))
    yield

    s_in = (u * jnp.concatenate(chunks, axis=0)).astype(BF16)
    merged_a = jax.nn.sigmoid(ga) * y_a
    yield

    gb = proj(HG_WIDTH + 2 * SG_WIDTH + D_MODEL, D_MODEL)
    y_b = _dot(s_in, wb_ref[...])
    yield

    merged = (merged_a + jax.nn.sigmoid(gb) * y_b).astype(BF16)
    yield

    mix = _dot(merged, wo_ref[...])
    yield

    x1_ref[rows, :] = x + _rms(mix, postw_ref[...])


def _mix_kernel(*refs):
    _run_skewed(_mix_stages(pl.ds(s * SUB_MIX, SUB_MIX), *refs)
                for s in range(TM_MIX // SUB_MIX))


def _mix(x2, o_f, o_b, pre_w, w_mix, hg_w, ln_w, ln_b, w_s, b_s_t, w_a, w_b, w_o, post_w):
    n_tok = x2.shape[0]
    return pl.pallas_call(
        _mix_kernel,
        grid=(n_tok // TM_MIX,),
        in_specs=[pl.BlockSpec((TM_MIX, D_MODEL), lambda i: (i, 0)),
                  pl.BlockSpec((TM_MIX, HG_WIDTH), lambda i: (i, 0)),
                  pl.BlockSpec((TM_MIX, HG_WIDTH), lambda i: (i, 0)),
                  _const_spec((1, D_MODEL)),
                  _const_spec((D_MODEL, MIX_COLS)),
                  _const_spec((1, HG_WIDTH)),
                  _const_spec((1, SG_WIDTH)),
                  _const_spec((1, SG_WIDTH)),
                  _const_spec((SG_GROUPS, SG_CHUNK, SG_CHUNK)),
                  _const_spec((SG_CHUNK, SG_GROUPS)),
                  _const_spec((HG_WIDTH, D_MODEL)),
                  _const_spec((SG_WIDTH, D_MODEL)),
                  _const_spec((D_MODEL, D_MODEL)),
                  _const_spec((1, D_MODEL))],
        out_specs=pl.BlockSpec((TM_MIX, D_MODEL), lambda i: (i, 0)),
        out_shape=jax.ShapeDtypeStruct((n_tok, D_MODEL), F32),
        compiler_params=pltpu.CompilerParams(
            dimension_semantics=("parallel",), vmem_limit_bytes=VMEM_LIMIT_BYTES),
        name="mixer_tail",
    )(x2, o_f, o_b, pre_w, w_mix, hg_w, ln_w, ln_b, w_s, b_s_t, w_a, w_b, w_o, post_w)


def _ffn_kernel(x_ref, prew_ref, wg_ref, wu_ref, wd_ref, postw_ref, out_ref):
    def stages(rows):
        x = x_ref[rows, :]
        h = _rms(x, prew_ref[...]).astype(BF16)
        acc = jnp.zeros((SUB_FFN, D_MODEL), F32)
        yield
        for t in range(D_FF // FF_TILE):
            cols = slice(t * FF_TILE, (t + 1) * FF_TILE)
            act = jax.nn.silu(_dot(h, wg_ref[:, cols])) * _dot(h, wu_ref[:, cols])
            acc = acc + _dot(act.astype(BF16), wd_ref[cols, :])
            yield
        out_ref[rows, :] = x + _rms(acc, postw_ref[...])

    _run_skewed(stages(pl.ds(s * SUB_FFN, SUB_FFN)) for s in range(TM_FFN // SUB_FFN))


def _ffn(x1, pre_w, w_g, w_u, w_d, post_w):
    n_tok = x1.shape[0]
    tile = pl.BlockSpec((TM_FFN, D_MODEL), lambda i: (i, 0))
    return pl.pallas_call(
        _ffn_kernel,
        grid=(n_tok // TM_FFN,),
        in_specs=[tile,
                  _const_spec((1, D_MODEL)),
                  _const_spec((D_MODEL, D_FF)),
                  _const_spec((D_MODEL, D_FF)),
                  _const_spec((D_FF, D_MODEL)),
                  _const_spec((1, D_MODEL))],
        out_specs=tile,
        out_shape=jax.ShapeDtypeStruct((n_tok, D_MODEL), F32),
        compiler_params=pltpu.CompilerParams(
            dimension_semantics=("parallel",), vmem_limit_bytes=VMEM_LIMIT_BYTES),
        name="swiglu_ffn",
    )(x1, pre_w, w_g, w_u, w_d, post_w)


def kernel(x, pre_mix_w, w_in, lb_logits, hg_norm_w, sg_ln_w, sg_ln_b, sg_spatial_w,
           sg_spatial_b, w_proj_a, w_proj_b, w_out, post_mix_w, pre_ffn_w, w_gate, w_up,
           w_down, post_ffn_w):
    bsz, seq, d = x.shape
    depth = pre_mix_w.shape[0]
    assert d == D_MODEL and seq % TB_SCAN == 0 and (bsz * seq) % TM_FFN == 0
    x2 = x.reshape(bsz * seq, d)
    lb_logits = lb_logits.astype(F32)
    for l in range(depth):
        w_in_b = w_in[l].astype(BF16)
        q, i, f_fw, f_bw = _hg_proj(x2, pre_mix_w[l][None], w_in_b[:, :HG_COLS])
        shp = (bsz, seq, HG_WIDTH)
        o_f, o_b = _hg_scan(lb_logits, q.reshape(shp), i.reshape(shp),
                            f_fw.reshape(shp), f_bw.reshape(shp), l)
        x1 = _mix(x2, o_f.reshape(-1, HG_WIDTH), o_b.reshape(-1, HG_WIDTH),
                  pre_mix_w[l][None], w_in_b[:, HG_COLS:], hg_norm_w[l][None],
                  sg_ln_w[l][None], sg_ln_b[l][None], sg_spatial_w[l].astype(BF16),
                  sg_spatial_b[l].T, w_proj_a[l].astype(BF16), w_proj_b[l].astype(BF16),
                  w_out[l].astype(BF16), post_mix_w[l][None])
        x2 = _ffn(x1, pre_ffn_w[l][None], w_gate[l].astype(BF16), w_up[l].astype(BF16),
                  w_down[l].astype(BF16), post_ffn_w[l][None])
    return x2.reshape(bsz, seq, d)
```

```python
import functools
from typing import Any, NamedTuple

import jax
import jax.numpy as jnp
from jax import lax
from jax.experimental import pallas as pl
from jax.experimental.pallas import tpu as pltpu

D_MODEL = 1024
HG_HEADS = 4
HG_HEAD_DIM = 128
HG_WIDTH = HG_HEADS * HG_HEAD_DIM
HG_CHUNK = 64
SG_GROUPS = 4
SG_GROUP_DIM = 128
SG_WIDTH = SG_GROUPS * SG_GROUP_DIM
SG_CHUNK = 128
D_FF = 2816
EPS = 1e-6

HG_COLS = 4 * HG_WIDTH
MIX_COLS = HG_WIDTH + 2 * SG_WIDTH + 2 * D_MODEL

VMEM_LIMIT_BYTES = 56 * 1024 * 1024

TB_SCAN = 512
N_CHUNKS = TB_SCAN // HG_CHUNK
PROJ_ROWS = 256
TM_MIX, SUB_MIX = 1024, 256
TM_FFN, SUB_FFN = 1024, 512
FF_TILE = 256

F32 = jnp.float32
BF16 = jnp.bfloat16

_NT = (((1,), (1,)), ((), ()))
_TN = (((0,), (0,)), ((), ()))


def _dot(a, b):
    return jnp.dot(a, b, preferred_element_type=F32)


def _rms(x, w):
    return x * lax.rsqrt(jnp.mean(x * x, axis=-1, keepdims=True) + EPS) * w


def _run_skewed(stage_iters):
    pending, live = list(stage_iters), []
    while pending or live:
        if pending:
            live.append(pending.pop(0))
        for it in list(live):
            if next(it, StopIteration) is StopIteration:
                live.remove(it)


def _interleave(main, fill):
    k = 0
    for idx, thunk in enumerate(main):
        while k < len(fill) and k * len(main) <= idx * len(fill):
            fill[k]()
            k += 1
        thunk()
    for thunk in fill[k:]:
        thunk()


def _const_spec(shape):
    zeros = (0,) * len(shape)
    return pl.BlockSpec(shape, lambda *_: zeros, pipeline_mode=pl.Buffered(1))


class _ScanScratch(NamedTuple):
    st: Any
    k: Any
    hi: Any
    lo: Any
    b: Any
    qr: Any
    kr: Any
    kd: Any
    qe: Any
    sc: Any
    kvt: Any
    dec: Any


def _scan_scratch_shapes():
    stage = (N_CHUNKS, HG_CHUNK, HG_WIDTH)
    per_head = (N_CHUNKS, HG_HEADS)
    return list(_ScanScratch(
        st=pltpu.VMEM((HG_HEADS, HG_HEAD_DIM, HG_HEAD_DIM), F32),
        k=pltpu.VMEM(stage, F32), hi=pltpu.VMEM(stage, BF16), lo=pltpu.VMEM(stage, BF16),
        b=pltpu.VMEM(stage, F32), qr=pltpu.VMEM(stage, BF16), kr=pltpu.VMEM(stage, BF16),
        kd=pltpu.VMEM(stage, BF16), qe=pltpu.VMEM(stage, BF16),
        sc=pltpu.VMEM(per_head + (HG_CHUNK, HG_CHUNK), BF16),
        kvt=pltpu.VMEM(per_head + (HG_HEAD_DIM, HG_HEAD_DIM), F32),
        dec=pltpu.VMEM((N_CHUNKS, 1, HG_WIDTH), F32)))


def _lower_bounds(lbl_ref, layer):
    logits = lbl_ref[...]
    e = jnp.exp(logits - jnp.max(logits, axis=0, keepdims=True))
    return jnp.sum(e[:layer + 1], axis=0) / jnp.sum(e, axis=0)


def _scan_thunks(reverse, lower, q_ref, v_ref, f_ref, o_ref, reset, s):
    row = lax.broadcasted_iota(jnp.int32, (HG_CHUNK, HG_CHUNK), 0)
    col = lax.broadcasted_iota(jnp.int32, (HG_CHUNK, HG_CHUNK), 1)
    mask = col >= row if reverse else col <= row
    tri = jnp.where(mask, 1.0, 0.0).astype(BF16)
    mid = HG_CHUNK // 2
    ref_row, last_row = (mid, 0) if reverse else (mid - 1, HG_CHUNK - 1)
    heads = [slice(h * HG_HEAD_DIM, (h + 1) * HG_HEAD_DIM) for h in range(HG_HEADS)]

    def rows(c):
        return pl.ds(c * HG_CHUNK, HG_CHUNK)

    def gates(c):
        f = lower + (1.0 - lower) * jax.nn.sigmoid(f_ref[rows(c), :])
        s.k[c] = 1.0 - f
        lf = jnp.log(f)
        hi = lf.astype(BF16)
        s.hi[c] = hi
        s.lo[c] = (lf - hi.astype(F32)).astype(BF16)

    def cumsum(c):
        s.b[c] = _dot(tri, s.hi[c]) + _dot(tri, s.lo[c])

    def decays(c):
        b = s.b[c]
        b_mid = b[ref_row:ref_row + 1, :]
        b_last = b[last_row:last_row + 1, :]
        qr = q_ref[rows(c), :] * jnp.exp(b - b_mid)
        kr = s.k[c] * jnp.exp(b_mid - b)
        s.qr[c] = qr.astype(BF16)
        s.kr[c] = kr.astype(BF16)
        s.qe[c] = (qr * jnp.exp(b_mid)).astype(BF16)
        s.kd[c] = (kr * jnp.exp(b_last - b_mid)).astype(BF16)
        s.dec[c] = jnp.exp(b_last)

    def scores(c):
        for h, sl in enumerate(heads):
            sc = lax.dot_general(s.qr[c, :, sl], s.kr[c, :, sl], _NT, preferred_element_type=F32)
            s.sc[c, h] = jnp.where(mask, sc, 0.0).astype(BF16)

    def intra(c):
        vb = v_ref[rows(c), :].astype(BF16)
        for h, sl in enumerate(heads):
            o_ref[rows(c), sl] = _dot(s.sc[c, h], vb[:, sl])
            s.kvt[c, h] = lax.dot_general(vb[:, sl], s.kd[c, :, sl], _TN,
                                          preferred_element_type=F32)

    def reset_state():
        s.st[...] = jnp.where(reset, 0.0, s.st[...])

    def serial(c):
        decay = s.dec[c]
        for h, sl in enumerate(heads):
            st = s.st[h]
            o_ref[rows(c), sl] += lax.dot_general(s.qe[c, :, sl], st.astype(BF16), _NT,
                                                  preferred_element_type=F32)
            s.st[h] = st * decay[:, sl] + s.kvt[c, h]

    chunks = list(range(N_CHUNKS))
    thunks = [functools.partial(phase, c)
              for phase in (gates, cumsum, decays, scores, intra) for c in chunks]
    if reset is not None:
        thunks.append(reset_state)
    return thunks + [functools.partial(serial, c) for c in (chunks[::-1] if reverse else chunks)]


def _proj_scan_kernel(lbl_ref, nw_ref, w_ref, x0_ref, xa_ref, xb_ref,
                      of_ref, qo_ref, io_ref, fbo_ref, p0_ref, p1_ref, h_ref, *scan_refs,
                      layer, steps_per_row):
    s = _ScanScratch(*scan_refs)
    u = pl.program_id(0)
    lower = _lower_bounds(lbl_ref, layer)[0:1, :]

    def project(x_ref, p_ref):
        thunks = []
        for r in range(TB_SCAN // PROJ_ROWS):
            rows = pl.ds(r * PROJ_ROWS, PROJ_ROWS)

            def norm(rows=rows):
                h_ref[rows, :] = _rms(x_ref[rows, :], nw_ref[...]).astype(BF16)

            def columns(n, rows=rows):
                p_ref[n, rows, :] = _dot(h_ref[rows, :], w_ref[:, n * HG_WIDTH:(n + 1) * HG_WIDTH])

            thunks.append(norm)
            thunks.extend(functools.partial(columns, n) for n in range(4))
        return thunks

    def publish(p_ref, rows):
        def q_out():
            qo_ref[rows, :] = p_ref[0]

        def i_out():
            io_ref[rows, :] = p_ref[1].astype(io_ref.dtype)

        def f_out():
            fbo_ref[rows, :] = p_ref[3]

        return [q_out, i_out, f_out]

    @pl.when(u == 0)
    def _():
        for thunk in project(x0_ref, p0_ref):
            thunk()

    halves = ((p0_ref, p1_ref, xa_ref, u % steps_per_row == 0), (p1_ref, p0_ref, xb_ref, None))
    for half, (cur, nxt, x_ref, reset) in enumerate(halves):
        rows = pl.ds(half * TB_SCAN, TB_SCAN)
        main = _scan_thunks(False, lower, cur.at[0], cur.at[1], cur.at[2], of_ref.at[rows, :],
                            reset, s)
        _interleave(main, project(x_ref, nxt) + publish(cur, rows))


def _proj_scan(x2, lb_logits, pre_w, w_hg, layer, seq):
    n_tok = x2.shape[0]
    n_blocks = n_tok // TB_SCAN
    n_steps = n_blocks // 2
    x_blk = (TB_SCAN, D_MODEL)
    out_blk = pl.BlockSpec((2 * TB_SCAN, HG_WIDTH), lambda u: (u, 0))
    f32_out = jax.ShapeDtypeStruct((n_tok, HG_WIDTH), F32)
    return pl.pallas_call(
        functools.partial(_proj_scan_kernel, layer=layer, steps_per_row=seq // (2 * TB_SCAN)),
        grid=(n_steps,),
        in_specs=[_const_spec(lb_logits.shape),
                  _const_spec((1, D_MODEL)),
                  _const_spec((D_MODEL, HG_COLS)),
                  pl.BlockSpec(x_blk, lambda u: (0, 0), pipeline_mode=pl.Buffered(1)),
                  pl.BlockSpec(x_blk, lambda u: (2 * u + 1, 0)),
                  pl.BlockSpec(x_blk, lambda u: (jnp.minimum(2 * u + 2, n_blocks - 1), 0))],
        out_specs=[out_blk] * 4,
        out_shape=[f32_out, f32_out, jax.ShapeDtypeStruct((n_tok, HG_WIDTH), BF16), f32_out],
        scratch_shapes=[pltpu.VMEM((4, TB_SCAN, HG_WIDTH), F32),
                        pltpu.VMEM((4, TB_SCAN, HG_WIDTH), F32),
                        pltpu.VMEM((TB_SCAN, D_MODEL), BF16)] + _scan_scratch_shapes(),
        compiler_params=pltpu.CompilerParams(
            dimension_semantics=("arbitrary",), vmem_limit_bytes=VMEM_LIMIT_BYTES),
        name="proj_fwd_scan",
    )(lb_logits, pre_w, w_hg, x2, x2, x2)


def _bwd_scan_kernel(lbl_ref, q_ref, i_ref, f_ref, o_ref, *scan_refs, layer):
    s = _ScanScratch(*scan_refs)
    lower = _lower_bounds(lbl_ref, layer)[1:2, :]
    for thunk in _scan_thunks(True, lower, q_ref, i_ref, f_ref, o_ref,
                              pl.program_id(1) == 0, s):
        thunk()


def _bwd_scan(lb_logits, q, i, f_bw, layer):
    bsz, seq, _ = q.shape
    nb = seq // TB_SCAN
    blk = pl.BlockSpec((None, TB_SCAN, HG_WIDTH), lambda b, j: (b, nb - 1 - j, 0))
    return pl.pallas_call(
        functools.partial(_bwd_scan_kernel, layer=layer),
        grid=(bsz, nb),
        in_specs=[_const_spec(lb_logits.shape), blk, blk, blk],
        out_specs=blk,
        out_shape=jax.ShapeDtypeStruct((bsz, seq, HG_WIDTH), F32),
        scratch_shapes=_scan_scratch_shapes(),
        compiler_params=pltpu.CompilerParams(
            dimension_semantics=("parallel", "arbitrary"), vmem_limit_bytes=VMEM_LIMIT_BYTES),
        name="bwd_scan",
    )(lb_logits, q, i, f_bw)


def _mix_stages(rows, x_ref, of_ref, ob_ref, prew_ref, win_ref, hgw_ref, lnw_ref, lnb_ref,
                ws_ref, bs_ref, wa_ref, wb_ref, wo_ref, postw_ref, x1_ref):
    x = x_ref[rows, :]
    h = _rms(x, prew_ref[...]).astype(BF16)

    def proj(lo, width):
        return _dot(h, win_ref[:, lo:lo + width])
    yield

    g = proj(0, HG_WIDTH)
    u = proj(HG_WIDTH, SG_WIDTH)
    v = proj(HG_WIDTH + SG_WIDTH, SG_WIDTH)
    yield

    o = of_ref[rows, :] + ob_ref[rows, :]
    heads = []
    for hd in range(HG_HEADS):
        oh = o[:, hd * HG_HEAD_DIM:(hd + 1) * HG_HEAD_DIM]
        heads.append(oh * lax.rsqrt(jnp.mean(oh * oh, axis=-1, keepdims=True) + EPS))
    o = jnp.concatenate(heads, axis=-1) * hgw_ref[...]
    a_in = (o * jax.nn.silu(g)).astype(BF16)
    u = jax.nn.gelu(u)
    v = jax.nn.gelu(v)
    mu = jnp.mean(v, axis=-1, keepdims=True)
    vc = v - mu
    v = vc * lax.rsqrt(jnp.mean(vc * vc, axis=-1, keepdims=True) + EPS)
    v = (v * lnw_ref[...] + lnb_ref[...]).astype(BF16)
    yield

    ga = proj(HG_WIDTH + 2 * SG_WIDTH, D_MODEL)
    y_a = _dot(a_in, wa_ref[...])
    chunks = []
    for c in range(SUB_MIX // SG_CHUNK):
        r = slice(c * SG_CHUNK, (c + 1) * SG_CHUNK)
        groups = []
        for gi in range(SG_GROUPS):
            sl = slice(gi * SG_GROUP_DIM, (gi + 1) * SG_GROUP_DIM)
            groups.append(_dot(ws_ref[gi], v[r, sl]) + bs_ref[:, gi:gi + 1])
        chunks.append(jnp.concatenate(groups, axis=-1))
    yield

    s_in = (u * jnp.concatenate(chunks, axis=0)).astype(BF16)
    merged_a = jax.nn.sigmoid(ga) * y_a
    yield

    gb = proj(HG_WIDTH + 2 * SG_WIDTH + D_MODEL, D_MODEL)
    y_b = _dot(s_in, wb_ref[...])
    yield

    merged = (merged_a + jax.nn.sigmoid(gb) * y_b).astype(BF16)
    yield

    mix = _dot(merged, wo_ref[...])
    yield

    x1_ref[rows, :] = x + _rms(mix, postw_ref[...])


def _mix_kernel(*refs):
    _run_skewed(_mix_stages(pl.ds(s * SUB_MIX, SUB_MIX), *refs)
                for s in range(TM_MIX // SUB_MIX))


def _mix(x2, o_f, o_b, pre_w, w_mix, hg_w, ln_w, ln_b, w_s, b_s_t, w_a, w_b, w_o, post_w):
    n_tok = x2.shape[0]
    return pl.pallas_call(
        _mix_kernel,
        grid=(n_tok // TM_MIX,),
        in_specs=[pl.BlockSpec((TM_MIX, D_MODEL), lambda i: (i, 0)),
                  pl.BlockSpec((TM_MIX, HG_WIDTH), lambda i: (i, 0)),
                  pl.BlockSpec((TM_MIX, HG_WIDTH), lambda i: (i, 0)),
                  _const_spec((1, D_MODEL)),
                  _const_spec((D_MODEL, MIX_COLS)),
                  _const_spec((1, HG_WIDTH)),
                  _const_spec((1, SG_WIDTH)),
                  _const_spec((1, SG_WIDTH)),
                  _const_spec((SG_GROUPS, SG_CHUNK, SG_CHUNK)),
                  _const_spec((SG_CHUNK, SG_GROUPS)),
                  _const_spec((HG_WIDTH, D_MODEL)),
                  _const_spec((SG_WIDTH, D_MODEL)),
                  _const_spec((D_MODEL, D_MODEL)),
                  _const_spec((1, D_MODEL))],
        out_specs=pl.BlockSpec((TM_MIX, D_MODEL), lambda i: (i, 0)),
        out_shape=jax.ShapeDtypeStruct((n_tok, D_MODEL), F32),
        compiler_params=pltpu.CompilerParams(
            dimension_semantics=("parallel",), vmem_limit_bytes=VMEM_LIMIT_BYTES),
        name="mixer_tail",
    )(x2, o_f, o_b, pre_w, w_mix, hg_w, ln_w, ln_b, w_s, b_s_t, w_a, w_b, w_o, post_w)


def _ffn_kernel(x_ref, prew_ref, wg_ref, wu_ref, wd_ref, postw_ref, out_ref):
    def stages(rows):
        x = x_ref[rows, :]
        h = _rms(x, prew_ref[...]).astype(BF16)
        acc = jnp.zeros((SUB_FFN, D_MODEL), F32)
        yield
        for t in range(D_FF // FF_TILE):
            cols = slice(t * FF_TILE, (t + 1) * FF_TILE)
            act = jax.nn.silu(_dot(h, wg_ref[:, cols])) * _dot(h, wu_ref[:, cols])
            acc = acc + _dot(act.astype(BF16), wd_ref[cols, :])
            yield
        out_ref[rows, :] = x + _rms(acc, postw_ref[...])

    _run_skewed(stages(pl.ds(s * SUB_FFN, SUB_FFN)) for s in range(TM_FFN // SUB_FFN))


def _ffn(x1, pre_w, w_g, w_u, w_d, post_w):
    n_tok = x1.shape[0]
    tile = pl.BlockSpec((TM_FFN, D_MODEL), lambda i: (i, 0))
    return pl.pallas_call(
        _ffn_kernel,
        grid=(n_tok // TM_FFN,),
        in_specs=[tile,
                  _const_spec((1, D_MODEL)),
                  _const_spec((D_MODEL, D_FF)),
                  _const_spec((D_MODEL, D_FF)),
                  _const_spec((D_FF, D_MODEL)),
                  _const_spec((1, D_MODEL))],
        out_specs=tile,
        out_shape=jax.ShapeDtypeStruct((n_tok, D_MODEL), F32),
        compiler_params=pltpu.CompilerParams(
            dimension_semantics=("parallel",), vmem_limit_bytes=VMEM_LIMIT_BYTES),
        name="swiglu_ffn",
    )(x1, pre_w, w_g, w_u, w_d, post_w)


def kernel(x, pre_mix_w, w_in, lb_logits, hg_norm_w, sg_ln_w, sg_ln_b, sg_spatial_w,
           sg_spatial_b, w_proj_a, w_proj_b, w_out, post_mix_w, pre_ffn_w, w_gate, w_up,
           w_down, post_ffn_w):
    bsz, seq, d = x.shape
    depth = pre_mix_w.shape[0]
    assert d == D_MODEL and seq % (2 * TB_SCAN) == 0 and (bsz * seq) % TM_FFN == 0
    x2 = x.reshape(bsz * seq, d)
    lb_logits = lb_logits.astype(F32)
    for l in range(depth):
        w_in_b = w_in[l].astype(BF16)
        o_f, q, i, f_bw = _proj_scan(x2, lb_logits, pre_mix_w[l][None], w_in_b[:, :HG_COLS],
                                     l, seq)
        shp = (bsz, seq, HG_WIDTH)
        o_b = _bwd_scan(lb_logits, q.reshape(shp), i.reshape(shp), f_bw.reshape(shp), l)
        x1 = _mix(x2, o_f, o_b.reshape(-1, HG_WIDTH),
                  pre_mix_w[l][None], w_in_b[:, HG_COLS:], hg_norm_w[l][None],
                  sg_ln_w[l][None], sg_ln_b[l][None], sg_spatial_w[l].astype(BF16),
                  sg_spatial_b[l].T, w_proj_a[l].astype(BF16), w_proj_b[l].astype(BF16),
                  w_out[l].astype(BF16), post_mix_w[l][None])
        x2 = _ffn(x1, pre_ffn_w[l][None], w_gate[l].astype(BF16), w_up[l].astype(BF16),
                  w_down[l].astype(BF16), post_ffn_w[l][None])
    return x2.reshape(bsz, seq, d)
```

```python
import functools
from typing import Any, NamedTuple

import jax
import jax.numpy as jnp
from jax import lax
from jax.experimental import pallas as pl
from jax.experimental.pallas import tpu as pltpu

D_MODEL = 1024
HG_HEADS = 4
HG_HEAD_DIM = 128
HG_WIDTH = HG_HEADS * HG_HEAD_DIM
HG_CHUNK = 64
SG_GROUPS = 4
SG_GROUP_DIM = 128
SG_WIDTH = SG_GROUPS * SG_GROUP_DIM
SG_CHUNK = 128
D_FF = 2816
EPS = 1e-6

HG_COLS = 4 * HG_WIDTH
MIX_COLS = HG_WIDTH + 2 * SG_WIDTH + 2 * D_MODEL

VMEM_LIMIT_BYTES = 56 * 1024 * 1024

TM_PROJ, SUB_PROJ = 1024, 512
TB_SCAN = 512
N_CHUNKS = TB_SCAN // HG_CHUNK
TM_MIX, SUB_MIX = 1024, 256
TM_FFN, SUB_FFN = 1024, 512
FF_TILE = 256

F32 = jnp.float32
BF16 = jnp.bfloat16

_NT = (((1,), (1,)), ((), ()))
_TN = (((0,), (0,)), ((), ()))


def _dot(a, b):
    return jnp.dot(a, b, preferred_element_type=F32)


def _rms(x, w):
    return x * lax.rsqrt(jnp.mean(x * x, axis=-1, keepdims=True) + EPS) * w


def _run_skewed(stage_iters):
    pending, live = list(stage_iters), []
    while pending or live:
        if pending:
            live.append(pending.pop(0))
        for it in list(live):
            if next(it, StopIteration) is StopIteration:
                live.remove(it)


def _interleave(main, fill):
    total = sum(weight for _, weight in main)
    done, k = 0, 0
    for thunk, weight in main:
        while k < len(fill) and k * total <= done * len(fill):
            fill[k]()
            k += 1
        thunk()
        done += weight
    for thunk in fill[k:]:
        thunk()


def _const_spec(shape):
    zeros = (0,) * len(shape)
    return pl.BlockSpec(shape, lambda *_: zeros, pipeline_mode=pl.Buffered(1))


def _hg_proj_kernel(x_ref, nw_ref, w_ref, q_ref, i_ref, ff_ref, fb_ref):
    def stages(rows):
        h = _rms(x_ref[rows, :], nw_ref[...]).astype(BF16)
        yield
        for n, ref in enumerate((q_ref, i_ref, ff_ref, fb_ref)):
            ref[rows, :] = _dot(h, w_ref[:, n * HG_WIDTH:(n + 1) * HG_WIDTH]).astype(ref.dtype)
            yield

    _run_skewed(stages(pl.ds(s * SUB_PROJ, SUB_PROJ)) for s in range(TM_PROJ // SUB_PROJ))


def _hg_proj(x2, pre_w, w_hg):
    n_tok = x2.shape[0]
    half = jax.ShapeDtypeStruct((n_tok, HG_WIDTH), BF16)
    full = jax.ShapeDtypeStruct((n_tok, HG_WIDTH), F32)
    tile = pl.BlockSpec((TM_PROJ, HG_WIDTH), lambda i: (i, 0))
    return pl.pallas_call(
        _hg_proj_kernel,
        grid=(n_tok // TM_PROJ,),
        in_specs=[pl.BlockSpec((TM_PROJ, D_MODEL), lambda i: (i, 0)),
                  _const_spec((1, D_MODEL)),
                  _const_spec((D_MODEL, HG_COLS))],
        out_specs=[tile] * 4,
        out_shape=[half, half, full, full],
        compiler_params=pltpu.CompilerParams(
            dimension_semantics=("parallel",), vmem_limit_bytes=VMEM_LIMIT_BYTES),
        name="hg_in_proj",
    )(x2, pre_w, w_hg)


class _ScanScratch(NamedTuple):
    st: Any
    k: Any
    hilo: Any
    b: Any
    qr: Any
    kr: Any
    kd: Any
    qe: Any
    dec: Any
    sc: Any
    kvt: Any


def _scan_scratch_shapes():
    stage = (2, N_CHUNKS, HG_CHUNK, HG_WIDTH)
    slots = (2,) + stage
    per_head = (2, N_CHUNKS, HG_HEADS)
    return list(_ScanScratch(
        st=pltpu.VMEM((2, HG_HEADS, HG_HEAD_DIM, HG_HEAD_DIM), F32),
        k=pltpu.VMEM(stage, F32),
        hilo=pltpu.VMEM((2, N_CHUNKS, 2 * HG_CHUNK, HG_WIDTH), BF16),
        b=pltpu.VMEM(stage, F32),
        qr=pltpu.VMEM(slots, BF16), kr=pltpu.VMEM(slots, BF16),
        kd=pltpu.VMEM(slots, BF16), qe=pltpu.VMEM(slots, BF16),
        dec=pltpu.VMEM((2, 2, N_CHUNKS, 1, HG_WIDTH), F32),
        sc=pltpu.VMEM(per_head + (HG_CHUNK, HG_CHUNK), BF16),
        kvt=pltpu.VMEM(per_head + (HG_HEAD_DIM, HG_HEAD_DIM), F32)))


_UNITS = [(d, c) for c in range(N_CHUNKS) for d in range(2)]
_HEADS = [slice(h * HG_HEAD_DIM, (h + 1) * HG_HEAD_DIM) for h in range(HG_HEADS)]


def _chunk_rows(c):
    return pl.ds(c * HG_CHUNK, HG_CHUNK)


def _scan_prepare(slot, lb, tris, q_refs, f_refs, s):
    mid = HG_CHUNK // 2
    ref_rows, last_rows = (mid - 1, mid), (HG_CHUNK - 1, 0)

    def gates(d, c):
        lower = lb[d:d + 1, :]
        f = lower + (1.0 - lower) * jax.nn.sigmoid(f_refs[d][_chunk_rows(c), :])
        s.k[d, c] = 1.0 - f
        lf = jnp.log(f)
        hi = lf.astype(BF16)
        s.hilo[d, c, :HG_CHUNK, :] = hi
        s.hilo[d, c, HG_CHUNK:, :] = (lf - hi.astype(F32)).astype(BF16)

    def cumsum(d, c):
        s.b[d, c] = _dot(tris[d], s.hilo[d, c])

    def decays(d, c):
        b = s.b[d, c]
        b_mid = b[ref_rows[d]:ref_rows[d] + 1, :]
        b_last = b[last_rows[d]:last_rows[d] + 1, :]
        qr = q_refs[d][_chunk_rows(c), :].astype(F32) * jnp.exp(b - b_mid)
        kr = s.k[d, c] * jnp.exp(b_mid - b)
        s.qr[slot, d, c] = qr.astype(BF16)
        s.kr[slot, d, c] = kr.astype(BF16)
        s.qe[slot, d, c] = (qr * jnp.exp(b_mid)).astype(BF16)
        s.kd[slot, d, c] = (kr * jnp.exp(b_last - b_mid)).astype(BF16)
        s.dec[slot, d, c] = jnp.exp(b_last)

    return [functools.partial(phase, d, c) for phase in (gates, cumsum, decays) for d, c in _UNITS]


def _scan_apply(slot, masks, v_refs, o_refs, reset, s):
    def scores(d, c):
        for h, sl in enumerate(_HEADS):
            sc = lax.dot_general(s.qr[slot, d, c, :, sl], s.kr[slot, d, c, :, sl], _NT,
                                 preferred_element_type=F32)
            s.sc[d, c, h] = jnp.where(masks[d], sc, 0.0).astype(BF16)

    def intra(d, c):
        vb = v_refs[d][_chunk_rows(c), :].astype(BF16)
        for h, sl in enumerate(_HEADS):
            o_refs[d][_chunk_rows(c), sl] = _dot(s.sc[d, c, h], vb[:, sl])
            s.kvt[d, c, h] = lax.dot_general(vb[:, sl], s.kd[slot, d, c, :, sl], _TN,
                                             preferred_element_type=F32)

    def reset_state():
        s.st[...] = jnp.where(reset, 0.0, s.st[...])

    def serial(d, c):
        decay = s.dec[slot, d, c]
        for h, sl in enumerate(_HEADS):
            st = s.st[d, h]
            o_refs[d][_chunk_rows(c), sl] += lax.dot_general(
                s.qe[slot, d, c, :, sl], st.astype(BF16), _NT, preferred_element_type=F32)
            s.st[d, h] = st * decay[:, sl] + s.kvt[d, c, h]

    thunks = [(functools.partial(phase, d, c), 1) for phase in (scores, intra) for d, c in _UNITS]
    if reset is not None:
        thunks.append((reset_state, 0))
    for step in range(N_CHUNKS):
        thunks += [(functools.partial(serial, 0, step), 2),
                   (functools.partial(serial, 1, N_CHUNKS - 1 - step), 2)]
    return thunks


def _scan_kernel(lbl_ref, q0f, f0f, q0b, f0b, qaf, faf, qbf, fbf, qab, fab, qbb, fbb,
                 vf_ref, vb_ref, of_ref, ob_ref, *scratch, layer, steps_per_row):
    s = _ScanScratch(*scratch)
    u = pl.program_id(0)
    lb = _lower_bounds(lbl_ref, layer)
    row = lax.broadcasted_iota(jnp.int32, (HG_CHUNK, HG_CHUNK), 0)
    col = lax.broadcasted_iota(jnp.int32, (HG_CHUNK, HG_CHUNK), 1)
    masks = (col <= row, col >= row)
    row2 = lax.broadcasted_iota(jnp.int32, (HG_CHUNK, 2 * HG_CHUNK), 0)
    col2 = lax.broadcasted_iota(jnp.int32, (HG_CHUNK, 2 * HG_CHUNK), 1) & (HG_CHUNK - 1)
    tris = tuple(jnp.where(m, 1.0, 0.0).astype(BF16) for m in (col2 <= row2, col2 >= row2))
    first, second = pl.ds(0, TB_SCAN), pl.ds(TB_SCAN, TB_SCAN)

    @pl.when(u == 0)
    def _():
        for thunk in _scan_prepare(0, lb, tris, (q0f, q0b), (f0f, f0b), s):
            thunk()

    _interleave(
        _scan_apply(0, masks, (vf_ref.at[first, :], vb_ref.at[second, :]),
                    (of_ref.at[first, :], ob_ref.at[second, :]), u % steps_per_row == 0, s),
        _scan_prepare(1, lb, tris, (qaf, qab), (faf, fab), s))
    _interleave(
        _scan_apply(1, masks, (vf_ref.at[second, :], vb_ref.at[first, :]),
                    (of_ref.at[second, :], ob_ref.at[first, :]), None, s),
        _scan_prepare(0, lb, tris, (qbf, qbb), (fbf, fbb), s))


def _lower_bounds(lbl_ref, layer):
    logits = lbl_ref[...]
    e = jnp.exp(logits - jnp.max(logits, axis=0, keepdims=True))
    return jnp.sum(e[:layer + 1], axis=0) / jnp.sum(e, axis=0)


def _hg_scan(lb_logits, q, i, f_fw, f_bw, layer):
    bsz, seq, _ = q.shape
    nb = seq // TB_SCAN
    spr = nb // 2

    def blk(index_map, **kw):
        return pl.BlockSpec((None, TB_SCAN, HG_WIDTH), index_map, **kw)

    def pair(index_map):
        return pl.BlockSpec((None, 2 * TB_SCAN, HG_WIDTH), index_map)

    def next_row(u):
        return jnp.minimum(u // spr + 1, bsz - 1)

    def last_in_row(u):
        return u % spr == spr - 1

    once = dict(pipeline_mode=pl.Buffered(1))
    first_f = blk(lambda u: (0, 0, 0), **once)
    first_b = blk(lambda u: (0, nb - 1, 0), **once)
    second_f = blk(lambda u: (u // spr, 2 * (u % spr) + 1, 0))
    second_b = blk(lambda u: (u // spr, nb - 2 - 2 * (u % spr), 0))
    next_f = blk(lambda u: (jnp.where(last_in_row(u), next_row(u), u // spr),
                            jnp.where(last_in_row(u), 0, 2 * (u % spr) + 2), 0))
    next_b = blk(lambda u: (jnp.where(last_in_row(u), next_row(u), u // spr),
                            jnp.where(last_in_row(u), nb - 1, nb - 3 - 2 * (u % spr)), 0))
    pair_f = pair(lambda u: (u // spr, u % spr, 0))
    pair_b = pair(lambda u: (u // spr, spr - 1 - u % spr, 0))
    out = jax.ShapeDtypeStruct((bsz, seq, HG_WIDTH), F32)
    return pl.pallas_call(
        functools.partial(_scan_kernel, layer=layer, steps_per_row=spr),
        grid=(bsz * spr,),
        in_specs=[_const_spec(lb_logits.shape),
                  first_f, first_f, first_b, first_b,
                  second_f, second_f, next_f, next_f,
                  second_b, second_b, next_b, next_b,
                  pair_f, pair_b],
        out_specs=[pair_f, pair_b],
        out_shape=[out, out],
        scratch_shapes=_scan_scratch_shapes(),
        compiler_params=pltpu.CompilerParams(
            dimension_semantics=("arbitrary",), vmem_limit_bytes=VMEM_LIMIT_BYTES),
        name="hg_scan",
    )(lb_logits, q, f_fw, q, f_bw, q, f_fw, q, f_fw, q, f_bw, q, f_bw, i, i)


def _mix_stages(rows, x_ref, of_ref, ob_ref, prew_ref, win_ref, hgw_ref, lnw_ref, lnb_ref,
                ws_ref, bs_ref, wa_ref, wb_ref, wo_ref, postw_ref, x1_ref):
    x = x_ref[rows, :]
    h = _rms(x, prew_ref[...]).astype(BF16)

    def proj(lo, width):
        return _dot(h, win_ref[:, lo:lo + width])
    yield

    g = proj(0, HG_WIDTH)
    u = proj(HG_WIDTH, SG_WIDTH)
    v = proj(HG_WIDTH + SG_WIDTH, SG_WIDTH)
    yield

    o = of_ref[rows, :] + ob_ref[rows, :]
    heads = []
    for hd in range(HG_HEADS):
        oh = o[:, hd * HG_HEAD_DIM:(hd + 1) * HG_HEAD_DIM]
        heads.append(oh * lax.rsqrt(jnp.mean(oh * oh, axis=-1, keepdims=True) + EPS))
    o = jnp.concatenate(heads, axis=-1) * hgw_ref[...]
    a_in = (o * jax.nn.silu(g)).astype(BF16)
    u = jax.nn.gelu(u)
    v = jax.nn.gelu(v)
    mu = jnp.mean(v, axis=-1, keepdims=True)
    vc = v - mu
    v = vc * lax.rsqrt(jnp.mean(vc * vc, axis=-1, keepdims=True) + EPS)
    v = (v * lnw_ref[...] + lnb_ref[...]).astype(BF16)
    yield

    ga = proj(HG_WIDTH + 2 * SG_WIDTH, D_MODEL)
    y_a = _dot(a_in, wa_ref[...])
    chunks = []
    for c in range(SUB_MIX // SG_CHUNK):
        r = slice(c * SG_CHUNK, (c + 1) * SG_CHUNK)
        groups = []
        for gi in range(SG_GROUPS):
            sl = slice(gi * SG_GROUP_DIM, (gi + 1) * SG_GROUP_DIM)
            groups.append(_dot(ws_ref[gi], v[r, sl]) + bs_ref[:, gi:gi + 1])
        chunks.append(jnp.concatenate(groups, axis=-1))
    yield

    s_in = (u * jnp.concatenate(chunks, axis=0)).astype(BF16)
    merged_a = jax.nn.sigmoid(ga) * y_a
    yield

    gb = proj(HG_WIDTH + 2 * SG_WIDTH + D_MODEL, D_MODEL)
    y_b = _dot(s_in, wb_ref[...])
    yield

    merged = (merged_a + jax.nn.sigmoid(gb) * y_b).astype(BF16)
    yield

    mix = _dot(merged, wo_ref[...])
    yield

    x1_ref[rows, :] = x + _rms(mix, postw_ref[...])


def _mix_kernel(*refs):
    _run_skewed(_mix_stages(pl.ds(s * SUB_MIX, SUB_MIX), *refs)
                for s in range(TM_MIX // SUB_MIX))


def _mix(x2, o_f, o_b, pre_w, w_mix, hg_w, ln_w, ln_b, w_s, b_s_t, w_a, w_b, w_o, post_w):
    n_tok = x2.shape[0]
    return pl.pallas_call(
        _mix_kernel,
        grid=(n_tok // TM_MIX,),
        in_specs=[pl.BlockSpec((TM_MIX, D_MODEL), lambda i: (i, 0)),
                  pl.BlockSpec((TM_MIX, HG_WIDTH), lambda i: (i, 0)),
                  pl.BlockSpec((TM_MIX, HG_WIDTH), lambda i: (i, 0)),
                  _const_spec((1, D_MODEL)),
                  _const_spec((D_MODEL, MIX_COLS)),
                  _const_spec((1, HG_WIDTH)),
                  _const_spec((1, SG_WIDTH)),
                  _const_spec((1, SG_WIDTH)),
                  _const_spec((SG_GROUPS, SG_CHUNK, SG_CHUNK)),
                  _const_spec((SG_CHUNK, SG_GROUPS)),
                  _const_spec((HG_WIDTH, D_MODEL)),
                  _const_spec((SG_WIDTH, D_MODEL)),
                  _const_spec((D_MODEL, D_MODEL)),
                  _const_spec((1, D_MODEL))],
        out_specs=pl.BlockSpec((TM_MIX, D_MODEL), lambda i: (i, 0)),
        out_shape=jax.ShapeDtypeStruct((n_tok, D_MODEL), F32),
        compiler_params=pltpu.CompilerParams(
            dimension_semantics=("parallel",), vmem_limit_bytes=VMEM_LIMIT_BYTES),
        name="mixer_tail",
    )(x2, o_f, o_b, pre_w, w_mix, hg_w, ln_w, ln_b, w_s, b_s_t, w_a, w_b, w_o, post_w)


def _ffn_kernel(x_ref, prew_ref, wg_ref, wu_ref, wd_ref, postw_ref, out_ref):
    def stages(rows):
        x = x_ref[rows, :]
        h = _rms(x, prew_ref[...]).astype(BF16)
        acc = jnp.zeros((SUB_FFN, D_MODEL), F32)
        yield
        for t in range(D_FF // FF_TILE):
            cols = slice(t * FF_TILE, (t + 1) * FF_TILE)
            act = jax.nn.silu(_dot(h, wg_ref[:, cols])) * _dot(h, wu_ref[:, cols])
            acc = acc + _dot(act.astype(BF16), wd_ref[cols, :])
            yield
        out_ref[rows, :] = x + _rms(acc, postw_ref[...])

    _run_skewed(stages(pl.ds(s * SUB_FFN, SUB_FFN)) for s in range(TM_FFN // SUB_FFN))


def _ffn(x1, pre_w, w_g, w_u, w_d, post_w):
    n_tok = x1.shape[0]
    tile = pl.BlockSpec((TM_FFN, D_MODEL), lambda i: (i, 0))
    return pl.pallas_call(
        _ffn_kernel,
        grid=(n_tok // TM_FFN,),
        in_specs=[tile,
                  _const_spec((1, D_MODEL)),
                  _const_spec((D_MODEL, D_FF)),
                  _const_spec((D_MODEL, D_FF)),
                  _const_spec((D_FF, D_MODEL)),
                  _const_spec((1, D_MODEL))],
        out_specs=tile,
        out_shape=jax.ShapeDtypeStruct((n_tok, D_MODEL), F32),
        compiler_params=pltpu.CompilerParams(
            dimension_semantics=("parallel",), vmem_limit_bytes=VMEM_LIMIT_BYTES),
        name="swiglu_ffn",
    )(x1, pre_w, w_g, w_u, w_d, post_w)


def kernel(x, pre_mix_w, w_in, lb_logits, hg_norm_w, sg_ln_w, sg_ln_b, sg_spatial_w,
           sg_spatial_b, w_proj_a, w_proj_b, w_out, post_mix_w, pre_ffn_w, w_gate, w_up,
           w_down, post_ffn_w):
    bsz, seq, d = x.shape
    depth = pre_mix_w.shape[0]
    assert d == D_MODEL and seq % (2 * TB_SCAN) == 0 and (bsz * seq) % TM_FFN == 0
    x2 = x.reshape(bsz * seq, d)
    lb_logits = lb_logits.astype(F32)
    for l in range(depth):
        w_in_b = w_in[l].astype(BF16)
        q, i, f_fw, f_bw = _hg_proj(x2, pre_mix_w[l][None], w_in_b[:, :HG_COLS])
        shp = (bsz, seq, HG_WIDTH)
        o_f, o_b = _hg_scan(lb_logits, q.reshape(shp), i.reshape(shp),
                            f_fw.reshape(shp), f_bw.reshape(shp), l)
        x1 = _mix(x2, o_f.reshape(-1, HG_WIDTH), o_b.reshape(-1, HG_WIDTH),
                  pre_mix_w[l][None], w_in_b[:, HG_COLS:], hg_norm_w[l][None],
                  sg_ln_w[l][None], sg_ln_b[l][None], sg_spatial_w[l].astype(BF16),
                  sg_spatial_b[l].T, w_proj_a[l].astype(BF16), w_proj_b[l].astype(BF16),
                  w_out[l].astype(BF16), post_mix_w[l][None])
        x2 = _ffn(x1, pre_ffn_w[l][None], w_gate[l].astype(BF16), w_up[l].astype(BF16),
                  w_down[l].astype(BF16), post_ffn_w[l][None])
    return x2.reshape(bsz, seq, d)
```

```python
import functools
from typing import Any, NamedTuple

import jax
import jax.numpy as jnp
from jax import lax
from jax.experimental import pallas as pl
from jax.experimental.pallas import tpu as pltpu

D_MODEL = 1024
HG_HEADS = 4
HG_HEAD_DIM = 128
HG_WIDTH = HG_HEADS * HG_HEAD_DIM
HG_CHUNK = 64
SG_GROUPS = 4
SG_GROUP_DIM = 128
SG_WIDTH = SG_GROUPS * SG_GROUP_DIM
SG_CHUNK = 128
D_FF = 2816
EPS = 1e-6
LOG2_E = 1.4426950408889634

HG_COLS = 4 * HG_WIDTH
MIX_COLS = HG_WIDTH + 2 * SG_WIDTH + 2 * D_MODEL

VMEM_LIMIT_BYTES = 56 * 1024 * 1024

TM_PROJ, SUB_PROJ = 1024, 512
TB_SCAN = 512
N_CHUNKS = TB_SCAN // HG_CHUNK
TM_MIX, SUB_MIX = 1024, 256
TM_FFN, SUB_FFN = 1024, 512
FF_TILE = 256

F32 = jnp.float32
BF16 = jnp.bfloat16

_NT = (((1,), (1,)), ((), ()))
_TN = (((0,), (0,)), ((), ()))


def _dot(a, b):
    return jnp.dot(a, b, preferred_element_type=F32)


def _rms(x, w):
    return x * lax.rsqrt(jnp.mean(x * x, axis=-1, keepdims=True) + EPS) * w


def _run_skewed(stage_iters):
    pending, live = list(stage_iters), []
    while pending or live:
        if pending:
            live.append(pending.pop(0))
        for it in list(live):
            if next(it, StopIteration) is StopIteration:
                live.remove(it)


def _interleave(main, fill):
    total = sum(weight for _, weight in main)
    done, k = 0, 0
    for thunk, weight in main:
        while k < len(fill) and k * total <= done * len(fill):
            fill[k]()
            k += 1
        thunk()
        done += weight
    for thunk in fill[k:]:
        thunk()


def _const_spec(shape):
    zeros = (0,) * len(shape)
    return pl.BlockSpec(shape, lambda *_: zeros, pipeline_mode=pl.Buffered(1))


def _hg_proj_kernel(x_ref, nw_ref, w_ref, q_ref, i_ref, ff_ref, fb_ref):
    def stages(rows):
        h = _rms(x_ref[rows, :], nw_ref[...]).astype(BF16)
        yield
        for n, ref in enumerate((q_ref, i_ref, ff_ref, fb_ref)):
            ref[rows, :] = _dot(h, w_ref[:, n * HG_WIDTH:(n + 1) * HG_WIDTH]).astype(ref.dtype)
            yield

    _run_skewed(stages(pl.ds(s * SUB_PROJ, SUB_PROJ)) for s in range(TM_PROJ // SUB_PROJ))


def _hg_proj(x2, pre_w, w_hg):
    n_tok = x2.shape[0]
    half = jax.ShapeDtypeStruct((n_tok, HG_WIDTH), BF16)
    full = jax.ShapeDtypeStruct((n_tok, HG_WIDTH), F32)
    tile = pl.BlockSpec((TM_PROJ, HG_WIDTH), lambda i: (i, 0))
    return pl.pallas_call(
        _hg_proj_kernel,
        grid=(n_tok // TM_PROJ,),
        in_specs=[pl.BlockSpec((TM_PROJ, D_MODEL), lambda i: (i, 0)),
                  _const_spec((1, D_MODEL)),
                  _const_spec((D_MODEL, HG_COLS))],
        out_specs=[tile] * 4,
        out_shape=[half, half, full, full],
        compiler_params=pltpu.CompilerParams(
            dimension_semantics=("parallel",), vmem_limit_bytes=VMEM_LIMIT_BYTES),
        name="hg_in_proj",
    )(x2, pre_w, w_hg)


class _ScanScratch(NamedTuple):
    st: Any
    k: Any
    hilo: Any
    b: Any
    qr: Any
    kr: Any
    kd: Any
    qe: Any
    dec: Any
    sc: Any


def _scan_scratch_shapes():
    stage = (2, N_CHUNKS, HG_CHUNK, HG_WIDTH)
    slots = (2,) + stage
    per_head = (2, N_CHUNKS, HG_HEADS)
    return list(_ScanScratch(
        st=pltpu.VMEM((2, HG_HEADS, HG_HEAD_DIM, HG_HEAD_DIM), F32),
        k=pltpu.VMEM(stage, F32),
        hilo=pltpu.VMEM((2, N_CHUNKS, 2 * HG_CHUNK, HG_WIDTH), BF16),
        b=pltpu.VMEM(stage, F32),
        qr=pltpu.VMEM(slots, BF16), kr=pltpu.VMEM(slots, BF16),
        kd=pltpu.VMEM(slots, BF16), qe=pltpu.VMEM(slots, BF16),
        dec=pltpu.VMEM((2, 2, N_CHUNKS, 1, HG_WIDTH), F32),
        sc=pltpu.VMEM(per_head + (HG_CHUNK, HG_CHUNK), BF16)))


_UNITS = [(d, c) for c in range(N_CHUNKS) for d in range(2)]
_HEADS = [slice(h * HG_HEAD_DIM, (h + 1) * HG_HEAD_DIM) for h in range(HG_HEADS)]


def _chunk_rows(c):
    return pl.ds(c * HG_CHUNK, HG_CHUNK)


def _scan_prepare(slot, lb, tris, q_refs, f_refs, s):
    mid = HG_CHUNK // 2
    ref_rows, last_rows = (mid - 1, mid), (HG_CHUNK - 1, 0)

    def gates(d, c):
        lower = lb[d:d + 1, :]
        f = lower + (1.0 - lower) * jax.nn.sigmoid(f_refs[d][_chunk_rows(c), :])
        s.k[d, c] = 1.0 - f
        lf = jnp.log(f)
        hi = lf.astype(BF16)
        s.hilo[d, c, :HG_CHUNK, :] = hi
        s.hilo[d, c, HG_CHUNK:, :] = (lf - hi.astype(F32)).astype(BF16)

    def cumsum(d, c):
        s.b[d, c] = _dot(tris[d], s.hilo[d, c])

    def decays(d, c):
        b = s.b[d, c]
        b_mid = b[ref_rows[d]:ref_rows[d] + 1, :]
        b_last = b[last_rows[d]:last_rows[d] + 1, :]
        t = (b - b_mid) * LOG2_E
        qr = q_refs[d][_chunk_rows(c), :].astype(F32) * jnp.exp2(t)
        kr = s.k[d, c] * jnp.exp2(-t)
        s.qr[slot, d, c] = qr.astype(BF16)
        s.kr[slot, d, c] = kr.astype(BF16)
        s.qe[slot, d, c] = (qr * jnp.exp(b_mid)).astype(BF16)
        s.kd[slot, d, c] = (kr * jnp.exp(b_last - b_mid)).astype(BF16)
        s.dec[slot, d, c] = jnp.exp(b_last)

    return [functools.partial(phase, d, c) for phase in (gates, cumsum, decays) for d, c in _UNITS]


def _scan_apply(slot, masks, v_refs, o_refs, reset, s):
    def scores(d, c):
        for h, sl in enumerate(_HEADS):
            sc = lax.dot_general(s.qr[slot, d, c, :, sl], s.kr[slot, d, c, :, sl], _NT,
                                 preferred_element_type=F32)
            s.sc[d, c, h] = jnp.where(masks[d], sc, 0.0).astype(BF16)

    def reset_state():
        s.st[...] = jnp.where(reset, 0.0, s.st[...])

    def output_and_state(d, c):
        decay = s.dec[slot, d, c]
        vb = v_refs[d][_chunk_rows(c), :].astype(BF16)
        for h, sl in enumerate(_HEADS):
            st = s.st[d, h]
            o_refs[d][_chunk_rows(c), sl] = _dot(s.sc[d, c, h], vb[:, sl]) + lax.dot_general(
                s.qe[slot, d, c, :, sl], st.astype(BF16), _NT, preferred_element_type=F32)
            s.st[d, h] = st * decay[:, sl] + lax.dot_general(
                vb[:, sl], s.kd[slot, d, c, :, sl], _TN, preferred_element_type=F32)

    thunks = [(functools.partial(scores, d, c), 1) for d, c in _UNITS]
    if reset is not None:
        thunks.append((reset_state, 0))
    for step in range(N_CHUNKS):
        thunks += [(functools.partial(output_and_state, 0, step), 3),
                   (functools.partial(output_and_state, 1, N_CHUNKS - 1 - step), 3)]
    return thunks


def _scan_kernel(lbl_ref, q0f, f0f, q0b, f0b, qaf, faf, qbf, fbf, qab, fab, qbb, fbb,
                 vf_ref, vb_ref, of_ref, ob_ref, *scratch, layer, steps_per_row):
    s = _ScanScratch(*scratch)
    u = pl.program_id(0)
    lb = _lower_bounds(lbl_ref, layer)
    row = lax.broadcasted_iota(jnp.int32, (HG_CHUNK, HG_CHUNK), 0)
    col = lax.broadcasted_iota(jnp.int32, (HG_CHUNK, HG_CHUNK), 1)
    masks = (col <= row, col >= row)
    row2 = lax.broadcasted_iota(jnp.int32, (HG_CHUNK, 2 * HG_CHUNK), 0)
    col2 = lax.broadcasted_iota(jnp.int32, (HG_CHUNK, 2 * HG_CHUNK), 1) & (HG_CHUNK - 1)
    tris = tuple(jnp.where(m, 1.0, 0.0).astype(BF16) for m in (col2 <= row2, col2 >= row2))
    first, second = pl.ds(0, TB_SCAN), pl.ds(TB_SCAN, TB_SCAN)

    @pl.when(u == 0)
    def _():
        for thunk in _scan_prepare(0, lb, tris, (q0f, q0b), (f0f, f0b), s):
            thunk()

    _interleave(
        _scan_apply(0, masks, (vf_ref.at[first, :], vb_ref.at[second, :]),
                    (of_ref.at[first, :], ob_ref.at[second, :]), u % steps_per_row == 0, s),
        _scan_prepare(1, lb, tris, (qaf, qab), (faf, fab), s))
    _interleave(
        _scan_apply(1, masks, (vf_ref.at[second, :], vb_ref.at[first, :]),
                    (of_ref.at[second, :], ob_ref.at[first, :]), None, s),
        _scan_prepare(0, lb, tris, (qbf, qbb), (fbf, fbb), s))


def _lower_bounds(lbl_ref, layer):
    logits = lbl_ref[...]
    e = jnp.exp(logits - jnp.max(logits, axis=0, keepdims=True))
    return jnp.sum(e[:layer + 1], axis=0) / jnp.sum(e, axis=0)


def _hg_scan(lb_logits, q, i, f_fw, f_bw, layer):
    bsz, seq, _ = q.shape
    nb = seq // TB_SCAN
    spr = nb // 2

    def blk(index_map, **kw):
        return pl.BlockSpec((None, TB_SCAN, HG_WIDTH), index_map, **kw)

    def pair(index_map):
        return pl.BlockSpec((None, 2 * TB_SCAN, HG_WIDTH), index_map)

    def next_row(u):
        return jnp.minimum(u // spr + 1, bsz - 1)

    def last_in_row(u):
        return u % spr == spr - 1

    once = dict(pipeline_mode=pl.Buffered(1))
    first_f = blk(lambda u: (0, 0, 0), **once)
    first_b = blk(lambda u: (0, nb - 1, 0), **once)
    second_f = blk(lambda u: (u // spr, 2 * (u % spr) + 1, 0))
    second_b = blk(lambda u: (u // spr, nb - 2 - 2 * (u % spr), 0))
    next_f = blk(lambda u: (jnp.where(last_in_row(u), next_row(u), u // spr),
                            jnp.where(last_in_row(u), 0, 2 * (u % spr) + 2), 0))
    next_b = blk(lambda u: (jnp.where(last_in_row(u), next_row(u), u // spr),
                            jnp.where(last_in_row(u), nb - 1, nb - 3 - 2 * (u % spr)), 0))
    pair_f = pair(lambda u: (u // spr, u % spr, 0))
    pair_b = pair(lambda u: (u // spr, spr - 1 - u % spr, 0))
    out = jax.ShapeDtypeStruct((bsz, seq, HG_WIDTH), F32)
    return pl.pallas_call(
        functools.partial(_scan_kernel, layer=layer, steps_per_row=spr),
        grid=(bsz * spr,),
        in_specs=[_const_spec(lb_logits.shape),
                  first_f, first_f, first_b, first_b,
                  second_f, second_f, next_f, next_f,
                  second_b, second_b, next_b, next_b,
                  pair_f, pair_b],
        out_specs=[pair_f, pair_b],
        out_shape=[out, out],
        scratch_shapes=_scan_scratch_shapes(),
        compiler_params=pltpu.CompilerParams(
            dimension_semantics=("arbitrary",), vmem_limit_bytes=VMEM_LIMIT_BYTES),
        name="hg_scan",
    )(lb_logits, q, f_fw, q, f_bw, q, f_fw, q, f_fw, q, f_bw, q, f_bw, i, i)


def _mix_stages(rows, x_ref, of_ref, ob_ref, prew_ref, win_ref, hgw_ref, lnw_ref, lnb_ref,
                ws_ref, bs_ref, wa_ref, wb_ref, wo_ref, postw_ref, x1_ref):
    x = x_ref[rows, :]
    h = _rms(x, prew_ref[...]).astype(BF16)

    def proj(lo, width):
        return _dot(h, win_ref[:, lo:lo + width])
    yield

    g = proj(0, HG_WIDTH)
    u = proj(HG_WIDTH, SG_WIDTH)
    v = proj(HG_WIDTH + SG_WIDTH, SG_WIDTH)
    yield

    o = of_ref[rows, :] + ob_ref[rows, :]
    heads = []
    for hd in range(HG_HEADS):
        oh = o[:, hd * HG_HEAD_DIM:(hd + 1) * HG_HEAD_DIM]
        heads.append(oh * lax.rsqrt(jnp.mean(oh * oh, axis=-1, keepdims=True) + EPS))
    o = jnp.concatenate(heads, axis=-1) * hgw_ref[...]
    a_in = (o * jax.nn.silu(g)).astype(BF16)
    u = jax.nn.gelu(u)
    v = jax.nn.gelu(v)
    mu = jnp.mean(v, axis=-1, keepdims=True)
    vc = v - mu
    v = vc * lax.rsqrt(jnp.mean(vc * vc, axis=-1, keepdims=True) + EPS)
    v = (v * lnw_ref[...] + lnb_ref[...]).astype(BF16)
    yield

    ga = proj(HG_WIDTH + 2 * SG_WIDTH, D_MODEL)
    y_a = _dot(a_in, wa_ref[...])
    chunks = []
    for c in range(SUB_MIX // SG_CHUNK):
        r = slice(c * SG_CHUNK, (c + 1) * SG_CHUNK)
        groups = []
        for gi in range(SG_GROUPS):
            sl = slice(gi * SG_GROUP_DIM, (gi + 1) * SG_GROUP_DIM)
            groups.append(_dot(ws_ref[gi], v[r, sl]) + bs_ref[:, gi:gi + 1])
        chunks.append(jnp.concatenate(groups, axis=-1))
    yield

    s_in = (u * jnp.concatenate(chunks, axis=0)).astype(BF16)
    merged_a = jax.nn.sigmoid(ga) * y_a
    yield

    gb = proj(HG_WIDTH + 2 * SG_WIDTH + D_MODEL, D_MODEL)
    y_b = _dot(s_in, wb_ref[...])
    yield

    merged = (merged_a + jax.nn.sigmoid(gb) * y_b).astype(BF16)
    yield

    mix = _dot(merged, wo_ref[...])
    yield

    x1_ref[rows, :] = x + _rms(mix, postw_ref[...])


def _mix_kernel(*refs):
    _run_skewed(_mix_stages(pl.ds(s * SUB_MIX, SUB_MIX), *refs)
                for s in range(TM_MIX // SUB_MIX))


def _mix(x2, o_f, o_b, pre_w, w_mix, hg_w, ln_w, ln_b, w_s, b_s_t, w_a, w_b, w_o, post_w):
    n_tok = x2.shape[0]
    return pl.pallas_call(
        _mix_kernel,
        grid=(n_tok // TM_MIX,),
        in_specs=[pl.BlockSpec((TM_MIX, D_MODEL), lambda i: (i, 0)),
                  pl.BlockSpec((TM_MIX, HG_WIDTH), lambda i: (i, 0)),
                  pl.BlockSpec((TM_MIX, HG_WIDTH), lambda i: (i, 0)),
                  _const_spec((1, D_MODEL)),
                  _const_spec((D_MODEL, MIX_COLS)),
                  _const_spec((1, HG_WIDTH)),
                  _const_spec((1, SG_WIDTH)),
                  _const_spec((1, SG_WIDTH)),
                  _const_spec((SG_GROUPS, SG_CHUNK, SG_CHUNK)),
                  _const_spec((SG_CHUNK, SG_GROUPS)),
                  _const_spec((HG_WIDTH, D_MODEL)),
                  _const_spec((SG_WIDTH, D_MODEL)),
                  _const_spec((D_MODEL, D_MODEL)),
                  _const_spec((1, D_MODEL))],
        out_specs=pl.BlockSpec((TM_MIX, D_MODEL), lambda i: (i, 0)),
        out_shape=jax.ShapeDtypeStruct((n_tok, D_MODEL), F32),
        compiler_params=pltpu.CompilerParams(
            dimension_semantics=("parallel",), vmem_limit_bytes=VMEM_LIMIT_BYTES),
        name="mixer_tail",
    )(x2, o_f, o_b, pre_w, w_mix, hg_w, ln_w, ln_b, w_s, b_s_t, w_a, w_b, w_o, post_w)


def _ffn_kernel(x_ref, prew_ref, wg_ref, wu_ref, wd_ref, postw_ref, out_ref):
    def stages(rows):
        x = x_ref[rows, :]
        h = _rms(x, prew_ref[...]).astype(BF16)
        acc = jnp.zeros((SUB_FFN, D_MODEL), F32)
        yield
        for t in range(D_FF // FF_TILE):
            cols = slice(t * FF_TILE, (t + 1) * FF_TILE)
            act = jax.nn.silu(_dot(h, wg_ref[:, cols])) * _dot(h, wu_ref[:, cols])
            acc = acc + _dot(act.astype(BF16), wd_ref[cols, :])
            yield
        out_ref[rows, :] = x + _rms(acc, postw_ref[...])

    _run_skewed(stages(pl.ds(s * SUB_FFN, SUB_FFN)) for s in range(TM_FFN // SUB_FFN))


def _ffn(x1, pre_w, w_g, w_u, w_d, post_w):
    n_tok = x1.shape[0]
    tile = pl.BlockSpec((TM_FFN, D_MODEL), lambda i: (i, 0))
    return pl.pallas_call(
        _ffn_kernel,
        grid=(n_tok // TM_FFN,),
        in_specs=[tile,
                  _const_spec((1, D_MODEL)),
                  _const_spec((D_MODEL, D_FF)),
                  _const_spec((D_MODEL, D_FF)),
                  _const_spec((D_FF, D_MODEL)),
                  _const_spec((1, D_MODEL))],
        out_specs=tile,
        out_shape=jax.ShapeDtypeStruct((n_tok, D_MODEL), F32),
        compiler_params=pltpu.CompilerParams(
            dimension_semantics=("parallel",), vmem_limit_bytes=VMEM_LIMIT_BYTES),
        name="swiglu_ffn",
    )(x1, pre_w, w_g, w_u, w_d, post_w)


def kernel(x, pre_mix_w, w_in, lb_logits, hg_norm_w, sg_ln_w, sg_ln_b, sg_spatial_w,
           sg_spatial_b, w_proj_a, w_proj_b, w_out, post_mix_w, pre_ffn_w, w_gate, w_up,
           w_down, post_ffn_w):
    bsz, seq, d = x.shape
    depth = pre_mix_w.shape[0]
    assert d == D_MODEL and seq % (2 * TB_SCAN) == 0 and (bsz * seq) % TM_FFN == 0
    x2 = x.reshape(bsz * seq, d)
    lb_logits = lb_logits.astype(F32)
    for l in range(depth):
        w_in_b = w_in[l].astype(BF16)
        q, i, f_fw, f_bw = _hg_proj(x2, pre_mix_w[l][None], w_in_b[:, :HG_COLS])
        shp = (bsz, seq, HG_WIDTH)
        o_f, o_b = _hg_scan(lb_logits, q.reshape(shp), i.reshape(shp),
                            f_fw.reshape(shp), f_bw.reshape(shp), l)
        x1 = _mix(x2, o_f.reshape(-1, HG_WIDTH), o_b.reshape(-1, HG_WIDTH),
                  pre_mix_w[l][None], w_in_b[:, HG_COLS:], hg_norm_w[l][None],
                  sg_ln_w[l][None], sg_ln_b[l][None], sg_spatial_w[l].astype(BF16),
                  sg_spatial_b[l].T, w_proj_a[l].astype(BF16), w_proj_b[l].astype(BF16),
                  w_out[l].astype(BF16), post_mix_w[l][None])
        x2 = _ffn(x1, pre_ffn_w[l][None], w_gate[l].astype(BF16), w_up[l].astype(BF16),
                  w_down[l].astype(BF16), post_ffn_w[l][None])
    return x2.reshape(bsz, seq, d)
```

```python
import functools
from typing import Any, NamedTuple

import jax
import jax.numpy as jnp
from jax import lax
from jax.experimental import pallas as pl
from jax.experimental.pallas import tpu as pltpu

D_MODEL = 1024
HG_HEADS = 4
HG_HEAD_DIM = 128
HG_WIDTH = HG_HEADS * HG_HEAD_DIM
HG_CHUNK = 64
SG_GROUPS = 4
SG_GROUP_DIM = 128
SG_WIDTH = SG_GROUPS * SG_GROUP_DIM
SG_CHUNK = 128
D_FF = 2816
EPS = 1e-6
LOG2_E = 1.4426950408889634

HG_COLS = 4 * HG_WIDTH
MIX_COLS = HG_WIDTH + 2 * SG_WIDTH + 2 * D_MODEL

VMEM_LIMIT_BYTES = 56 * 1024 * 1024

TM_PROJ, SUB_PROJ = 1024, 512
TB_SCAN = 512
N_CHUNKS = TB_SCAN // HG_CHUNK
TM_MIX, SUB_MIX = 1024, 256
TM_FFN, SUB_FFN = 1024, 512
FF_TILE = 256

F32 = jnp.float32
BF16 = jnp.bfloat16

_NT = (((1,), (1,)), ((), ()))
_TN = (((0,), (0,)), ((), ()))


def _dot(a, b):
    return jnp.dot(a, b, preferred_element_type=F32)


def _rms(x, w):
    return x * lax.rsqrt(jnp.mean(x * x, axis=-1, keepdims=True) + EPS) * w


def _run_skewed(stage_iters):
    pending, live = list(stage_iters), []
    while pending or live:
        if pending:
            live.append(pending.pop(0))
        for it in list(live):
            if next(it, StopIteration) is StopIteration:
                live.remove(it)


def _interleave(main, fill):
    k = 0
    for idx, thunk in enumerate(main):
        while k < len(fill) and k * len(main) <= idx * len(fill):
            fill[k]()
            k += 1
        thunk()
    for thunk in fill[k:]:
        thunk()


def _const_spec(shape):
    zeros = (0,) * len(shape)
    return pl.BlockSpec(shape, lambda *_: zeros, pipeline_mode=pl.Buffered(1))


def _hg_proj_kernel(x_ref, nw_ref, w_ref, q_ref, i_ref, ff_ref, fb_ref):
    def stages(rows):
        h = _rms(x_ref[rows, :], nw_ref[...]).astype(BF16)
        yield
        for n, ref in enumerate((q_ref, i_ref, ff_ref, fb_ref)):
            ref[rows, :] = _dot(h, w_ref[:, n * HG_WIDTH:(n + 1) * HG_WIDTH]).astype(ref.dtype)
            yield

    _run_skewed(stages(pl.ds(s * SUB_PROJ, SUB_PROJ)) for s in range(TM_PROJ // SUB_PROJ))


def _hg_proj(x2, pre_w, w_hg):
    n_tok = x2.shape[0]
    half = jax.ShapeDtypeStruct((n_tok, HG_WIDTH), BF16)
    full = jax.ShapeDtypeStruct((n_tok, HG_WIDTH), F32)
    tile = pl.BlockSpec((TM_PROJ, HG_WIDTH), lambda i: (i, 0))
    return pl.pallas_call(
        _hg_proj_kernel,
        grid=(n_tok // TM_PROJ,),
        in_specs=[pl.BlockSpec((TM_PROJ, D_MODEL), lambda i: (i, 0)),
                  _const_spec((1, D_MODEL)),
                  _const_spec((D_MODEL, HG_COLS))],
        out_specs=[tile] * 4,
        out_shape=[half, half, full, full],
        compiler_params=pltpu.CompilerParams(
            dimension_semantics=("parallel",), vmem_limit_bytes=VMEM_LIMIT_BYTES),
        name="hg_in_proj",
    )(x2, pre_w, w_hg)


class _ScanScratch(NamedTuple):
    st: Any
    k: Any
    b: Any
    kd: Any
    qe: Any
    dec: Any
    sc: Any


def _scan_scratch_shapes():
    stage = (2, N_CHUNKS, HG_CHUNK, HG_WIDTH)
    slots = (2,) + stage
    return list(_ScanScratch(
        st=pltpu.VMEM((2, HG_HEADS, HG_HEAD_DIM, HG_HEAD_DIM), F32),
        k=pltpu.VMEM(stage, F32),
        b=pltpu.VMEM(stage, F32),
        kd=pltpu.VMEM(slots, BF16), qe=pltpu.VMEM(slots, BF16),
        dec=pltpu.VMEM((2, 2, N_CHUNKS, 1, HG_WIDTH), F32),
        sc=pltpu.VMEM((2, 2, N_CHUNKS, HG_HEADS, HG_CHUNK, HG_CHUNK), BF16)))


_UNITS = [(d, c) for c in range(N_CHUNKS) for d in range(2)]
_HEADS = [slice(h * HG_HEAD_DIM, (h + 1) * HG_HEAD_DIM) for h in range(HG_HEADS)]


def _chunk_rows(c):
    return pl.ds(c * HG_CHUNK, HG_CHUNK)


def _scan_prepare(slot, lb, masks, tris, q_refs, f_refs, s):
    mid = HG_CHUNK // 2
    ref_rows, last_rows = (mid - 1, mid), (HG_CHUNK - 1, 0)

    def gates(d, c):
        lower = lb[d:d + 1, :]
        f = lower + (1.0 - lower) * jax.nn.sigmoid(f_refs[d][_chunk_rows(c), :])
        s.k[d, c] = 1.0 - f
        lf = jnp.log(f)
        hi = lf.astype(BF16)
        lo = (lf - hi.astype(F32)).astype(BF16)
        s.b[d, c] = _dot(tris[d], jnp.concatenate([hi, lo], axis=0))

    def decays(d, c):
        b = s.b[d, c]
        b_mid = b[ref_rows[d]:ref_rows[d] + 1, :]
        b_last = b[last_rows[d]:last_rows[d] + 1, :]
        t = (b - b_mid) * LOG2_E
        qr = q_refs[d][_chunk_rows(c), :].astype(F32) * jnp.exp2(t)
        kr = s.k[d, c] * jnp.exp2(-t)
        s.qe[slot, d, c] = (qr * jnp.exp(b_mid)).astype(BF16)
        s.kd[slot, d, c] = (kr * jnp.exp(b_last - b_mid)).astype(BF16)
        s.dec[slot, d, c] = jnp.exp(b_last)
        qr = qr.astype(BF16)
        kr = kr.astype(BF16)
        for h, sl in enumerate(_HEADS):
            sc = lax.dot_general(qr[:, sl], kr[:, sl], _NT, preferred_element_type=F32)
            s.sc[slot, d, c, h] = jnp.where(masks[d], sc, 0.0).astype(BF16)

    return [functools.partial(phase, d, c) for phase in (gates, decays) for d, c in _UNITS]


def _scan_apply(slot, v_refs, o_refs, reset, s):
    def reset_state():
        s.st[...] = jnp.where(reset, 0.0, s.st[...])

    def output_and_state(d, c):
        decay = s.dec[slot, d, c]
        vb = v_refs[d][_chunk_rows(c), :].astype(BF16)
        for h, sl in enumerate(_HEADS):
            st = s.st[d, h]
            o_refs[d][_chunk_rows(c), sl] = _dot(s.sc[slot, d, c, h], vb[:, sl]) + lax.dot_general(
                s.qe[slot, d, c, :, sl], st.astype(BF16), _NT, preferred_element_type=F32)
            s.st[d, h] = st * decay[:, sl] + lax.dot_general(
                vb[:, sl], s.kd[slot, d, c, :, sl], _TN, preferred_element_type=F32)

    thunks = [] if reset is None else [reset_state]
    for step in range(N_CHUNKS):
        thunks += [functools.partial(output_and_state, 0, step),
                   functools.partial(output_and_state, 1, N_CHUNKS - 1 - step)]
    return thunks


def _scan_kernel(lbl_ref, q0f, f0f, q0b, f0b, qaf, faf, qbf, fbf, qab, fab, qbb, fbb,
                 vf_ref, vb_ref, of_ref, ob_ref, *scratch, layer, steps_per_row):
    s = _ScanScratch(*scratch)
    u = pl.program_id(0)
    lb = _lower_bounds(lbl_ref, layer)
    row = lax.broadcasted_iota(jnp.int32, (HG_CHUNK, HG_CHUNK), 0)
    col = lax.broadcasted_iota(jnp.int32, (HG_CHUNK, HG_CHUNK), 1)
    masks = (col <= row, col >= row)
    row2 = lax.broadcasted_iota(jnp.int32, (HG_CHUNK, 2 * HG_CHUNK), 0)
    col2 = lax.broadcasted_iota(jnp.int32, (HG_CHUNK, 2 * HG_CHUNK), 1) & (HG_CHUNK - 1)
    tris = tuple(jnp.where(m, 1.0, 0.0).astype(BF16) for m in (col2 <= row2, col2 >= row2))
    first, second = pl.ds(0, TB_SCAN), pl.ds(TB_SCAN, TB_SCAN)

    @pl.when(u == 0)
    def _():
        for thunk in _scan_prepare(0, lb, masks, tris, (q0f, q0b), (f0f, f0b), s):
            thunk()

    _interleave(
        _scan_prepare(1, lb, masks, tris, (qaf, qab), (faf, fab), s),
        _scan_apply(0, (vf_ref.at[first, :], vb_ref.at[second, :]),
                    (of_ref.at[first, :], ob_ref.at[second, :]), u % steps_per_row == 0, s))
    _interleave(
        _scan_prepare(0, lb, masks, tris, (qbf, qbb), (fbf, fbb), s),
        _scan_apply(1, (vf_ref.at[second, :], vb_ref.at[first, :]),
                    (of_ref.at[second, :], ob_ref.at[first, :]), None, s))


def _lower_bounds(lbl_ref, layer):
    logits = lbl_ref[...]
    e = jnp.exp(logits - jnp.max(logits, axis=0, keepdims=True))
    return jnp.sum(e[:layer + 1], axis=0) / jnp.sum(e, axis=0)


def _hg_scan(lb_logits, q, i, f_fw, f_bw, layer):
    bsz, seq, _ = q.shape
    nb = seq // TB_SCAN
    spr = nb // 2

    def blk(index_map, **kw):
        return pl.BlockSpec((None, TB_SCAN, HG_WIDTH), index_map, **kw)

    def pair(index_map):
        return pl.BlockSpec((None, 2 * TB_SCAN, HG_WIDTH), index_map)

    def next_row(u):
        return jnp.minimum(u // spr + 1, bsz - 1)

    def last_in_row(u):
        return u % spr == spr - 1

    once = dict(pipeline_mode=pl.Buffered(1))
    first_f = blk(lambda u: (0, 0, 0), **once)
    first_b = blk(lambda u: (0, nb - 1, 0), **once)
    second_f = blk(lambda u: (u // spr, 2 * (u % spr) + 1, 0))
    second_b = blk(lambda u: (u // spr, nb - 2 - 2 * (u % spr), 0))
    next_f = blk(lambda u: (jnp.where(last_in_row(u), next_row(u), u // spr),
                            jnp.where(last_in_row(u), 0, 2 * (u % spr) + 2), 0))
    next_b = blk(lambda u: (jnp.where(last_in_row(u), next_row(u), u // spr),
                            jnp.where(last_in_row(u), nb - 1, nb - 3 - 2 * (u % spr)), 0))
    pair_f = pair(lambda u: (u // spr, u % spr, 0))
    pair_b = pair(lambda u: (u // spr, spr - 1 - u % spr, 0))
    out = jax.ShapeDtypeStruct((bsz, seq, HG_WIDTH), F32)
    return pl.pallas_call(
        functools.partial(_scan_kernel, layer=layer, steps_per_row=spr),
        grid=(bsz * spr,),
        in_specs=[_const_spec(lb_logits.shape),
                  first_f, first_f, first_b, first_b,
                  second_f, second_f, next_f, next_f,
                  second_b, second_b, next_b, next_b,
                  pair_f, pair_b],
        out_specs=[pair_f, pair_b],
        out_shape=[out, out],
        scratch_shapes=_scan_scratch_shapes(),
        compiler_params=pltpu.CompilerParams(
            dimension_semantics=("arbitrary",), vmem_limit_bytes=VMEM_LIMIT_BYTES),
        name="hg_scan",
    )(lb_logits, q, f_fw, q, f_bw, q, f_fw, q, f_fw, q, f_bw, q, f_bw, i, i)


def _mix_stages(rows, x_ref, of_ref, ob_ref, prew_ref, win_ref, hgw_ref, lnw_ref, lnb_ref,
                ws_ref, bs_ref, wa_ref, wb_ref, wo_ref, postw_ref, x1_ref):
    x = x_ref[rows, :]
    h = _rms(x, prew_ref[...]).astype(BF16)

    def proj(lo, width):
        return _dot(h, win_ref[:, lo:lo + width])
    yield

    g = proj(0, HG_WIDTH)
    u = proj(HG_WIDTH, SG_WIDTH)
    v = proj(HG_WIDTH + SG_WIDTH, SG_WIDTH)
    yield

    o = of_ref[rows, :] + ob_ref[rows, :]
    heads = []
    for hd in range(HG_HEADS):
        oh = o[:, hd * HG_HEAD_DIM:(hd + 1) * HG_HEAD_DIM]
        heads.append(oh * lax.rsqrt(jnp.mean(oh * oh, axis=-1, keepdims=True) + EPS))
    o = jnp.concatenate(heads, axis=-1) * hgw_ref[...]
    a_in = (o * jax.nn.silu(g)).astype(BF16)
    u = jax.nn.gelu(u)
    v = jax.nn.gelu(v)
    mu = jnp.mean(v, axis=-1, keepdims=True)
    vc = v - mu
    v = vc * lax.rsqrt(jnp.mean(vc * vc, axis=-1, keepdims=True) + EPS)
    v = (v * lnw_ref[...] + lnb_ref[...]).astype(BF16)
    yield

    ga = proj(HG_WIDTH + 2 * SG_WIDTH, D_MODEL)
    y_a = _dot(a_in, wa_ref[...])
    chunks = []
    for c in range(SUB_MIX // SG_CHUNK):
        r = slice(c * SG_CHUNK, (c + 1) * SG_CHUNK)
        groups = []
        for gi in range(SG_GROUPS):
            sl = slice(gi * SG_GROUP_DIM, (gi + 1) * SG_GROUP_DIM)
            groups.append(_dot(ws_ref[gi], v[r, sl]) + bs_ref[:, gi:gi + 1])
        chunks.append(jnp.concatenate(groups, axis=-1))
    yield

    s_in = (u * jnp.concatenate(chunks, axis=0)).astype(BF16)
    merged_a = jax.nn.sigmoid(ga) * y_a
    yield

    gb = proj(HG_WIDTH + 2 * SG_WIDTH + D_MODEL, D_MODEL)
    y_b = _dot(s_in, wb_ref[...])
    yield

    merged = (merged_a + jax.nn.sigmoid(gb) * y_b).astype(BF16)
    yield

    mix = _dot(merged, wo_ref[...])
    yield

    x1_ref[rows, :] = x + _rms(mix, postw_ref[...])


def _mix_kernel(*refs):
    _run_skewed(_mix_stages(pl.ds(s * SUB_MIX, SUB_MIX), *refs)
                for s in range(TM_MIX // SUB_MIX))


def _mix(x2, o_f, o_b, pre_w, w_mix, hg_w, ln_w, ln_b, w_s, b_s_t, w_a, w_b, w_o, post_w):
    n_tok = x2.shape[0]
    return pl.pallas_call(
        _mix_kernel,
        grid=(n_tok // TM_MIX,),
        in_specs=[pl.BlockSpec((TM_MIX, D_MODEL), lambda i: (i, 0)),
                  pl.BlockSpec((TM_MIX, HG_WIDTH), lambda i: (i, 0)),
                  pl.BlockSpec((TM_MIX, HG_WIDTH), lambda i: (i, 0)),
                  _const_spec((1, D_MODEL)),
                  _const_spec((D_MODEL, MIX_COLS)),
                  _const_spec((1, HG_WIDTH)),
                  _const_spec((1, SG_WIDTH)),
                  _const_spec((1, SG_WIDTH)),
                  _const_spec((SG_GROUPS, SG_CHUNK, SG_CHUNK)),
                  _const_spec((SG_CHUNK, SG_GROUPS)),
                  _const_spec((HG_WIDTH, D_MODEL)),
                  _const_spec((SG_WIDTH, D_MODEL)),
                  _const_spec((D_MODEL, D_MODEL)),
                  _const_spec((1, D_MODEL))],
        out_specs=pl.BlockSpec((TM_MIX, D_MODEL), lambda i: (i, 0)),
        out_shape=jax.ShapeDtypeStruct((n_tok, D_MODEL), F32),
        compiler_params=pltpu.CompilerParams(
            dimension_semantics=("parallel",), vmem_limit_bytes=VMEM_LIMIT_BYTES),
        name="mixer_tail",
    )(x2, o_f, o_b, pre_w, w_mix, hg_w, ln_w, ln_b, w_s, b_s_t, w_a, w_b, w_o, post_w)


def _ffn_kernel(x_ref, prew_ref, wg_ref, wu_ref, wd_ref, postw_ref, out_ref):
    def stages(rows):
        x = x_ref[rows, :]
        h = _rms(x, prew_ref[...]).astype(BF16)
        acc = jnp.zeros((SUB_FFN, D_MODEL), F32)
        yield
        for t in range(D_FF // FF_TILE):
            cols = slice(t * FF_TILE, (t + 1) * FF_TILE)
            act = jax.nn.silu(_dot(h, wg_ref[:, cols])) * _dot(h, wu_ref[:, cols])
            acc = acc + _dot(act.astype(BF16), wd_ref[cols, :])
            yield
        out_ref[rows, :] = x + _rms(acc, postw_ref[...])

    _run_skewed(stages(pl.ds(s * SUB_FFN, SUB_FFN)) for s in range(TM_FFN // SUB_FFN))


def _ffn(x1, pre_w, w_g, w_u, w_d, post_w):
    n_tok = x1.shape[0]
    tile = pl.BlockSpec((TM_FFN, D_MODEL), lambda i: (i, 0))
    return pl.pallas_call(
        _ffn_kernel,
        grid=(n_tok // TM_FFN,),
        in_specs=[tile,
                  _const_spec((1, D_MODEL)),
                  _const_spec((D_MODEL, D_FF)),
                  _const_spec((D_MODEL, D_FF)),
                  _const_spec((D_FF, D_MODEL)),
                  _const_spec((1, D_MODEL))],
        out_specs=tile,
        out_shape=jax.ShapeDtypeStruct((n_tok, D_MODEL), F32),
        compiler_params=pltpu.CompilerParams(
            dimension_semantics=("parallel",), vmem_limit_bytes=VMEM_LIMIT_BYTES),
        name="swiglu_ffn",
    )(x1, pre_w, w_g, w_u, w_d, post_w)


def kernel(x, pre_mix_w, w_in, lb_logits, hg_norm_w, sg_ln_w, sg_ln_b, sg_spatial_w,
           sg_spatial_b, w_proj_a, w_proj_b, w_out, post_mix_w, pre_ffn_w, w_gate, w_up,
           w_down, post_ffn_w):
    bsz, seq, d = x.shape
    depth = pre_mix_w.shape[0]
    assert d == D_MODEL and seq % (2 * TB_SCAN) == 0 and (bsz * seq) % TM_FFN == 0
    x2 = x.reshape(bsz * seq, d)
    lb_logits = lb_logits.astype(F32)
    for l in range(depth):
        w_in_b = w_in[l].astype(BF16)
        q, i, f_fw, f_bw = _hg_proj(x2, pre_mix_w[l][None], w_in_b[:, :HG_COLS])
        shp = (bsz, seq, HG_WIDTH)
        o_f, o_b = _hg_scan(lb_logits, q.reshape(shp), i.reshape(shp),
                            f_fw.reshape(shp), f_bw.reshape(shp), l)
        x1 = _mix(x2, o_f.reshape(-1, HG_WIDTH), o_b.reshape(-1, HG_WIDTH),
                  pre_mix_w[l][None], w_in_b[:, HG_COLS:], hg_norm_w[l][None],
                  sg_ln_w[l][None], sg_ln_b[l][None], sg_spatial_w[l].astype(BF16),
                  sg_spatial_b[l].T, w_proj_a[l].astype(BF16), w_proj_b[l].astype(BF16),
                  w_out[l].astype(BF16), post_mix_w[l][None])
        x2 = _ffn(x1, pre_ffn_w[l][None], w_gate[l].astype(BF16), w_up[l].astype(BF16),
                  w_down[l].astype(BF16), post_ffn_w[l][None])
    return x2.reshape(bsz, seq, d)
```

```python
import functools
from typing import Any, NamedTuple

import jax
import jax.numpy as jnp
from jax import lax
from jax.experimental import pallas as pl
from jax.experimental.pallas import tpu as pltpu

D_MODEL = 1024
HG_HEADS = 4
HG_HEAD_DIM = 128
HG_WIDTH = HG_HEADS * HG_HEAD_DIM
HG_CHUNK = 64
SG_GROUPS = 4
SG_GROUP_DIM = 128
SG_WIDTH = SG_GROUPS * SG_GROUP_DIM
SG_CHUNK = 128
D_FF = 2816
EPS = 1e-6
LOG2_E = 1.4426950408889634

HG_COLS = 4 * HG_WIDTH
MIX_COLS = HG_WIDTH + 2 * SG_WIDTH + 2 * D_MODEL

VMEM_LIMIT_BYTES = 56 * 1024 * 1024

TM_PROJ, SUB_PROJ = 1024, 512
TB_SCAN = 512
N_CHUNKS = TB_SCAN // HG_CHUNK
TM_MIX, SUB_MIX = 1024, 256
TM_FFN, SUB_FFN = 1024, 512
FF_TILE = 256

F32 = jnp.float32
BF16 = jnp.bfloat16
BF16_SUBLANES = 16

_NT = (((1,), (1,)), ((), ()))
_TN = (((0,), (0,)), ((), ()))


def _dot(a, b):
    return jnp.dot(a, b, preferred_element_type=F32)


def _rms(x, w):
    return x * lax.rsqrt(jnp.mean(x * x, axis=-1, keepdims=True) + EPS) * w


def _run_skewed(stage_iters):
    pending, live = list(stage_iters), []
    while pending or live:
        if pending:
            live.append(pending.pop(0))
        for it in list(live):
            if next(it, StopIteration) is StopIteration:
                live.remove(it)


def _interleave(main, fill):
    k = 0
    for idx, thunk in enumerate(main):
        while k < len(fill) and k * len(main) <= idx * len(fill):
            fill[k]()
            k += 1
        thunk()
    for thunk in fill[k:]:
        thunk()


def _const_spec(shape):
    zeros = (0,) * len(shape)
    return pl.BlockSpec(shape, lambda *_: zeros, pipeline_mode=pl.Buffered(1))


def _hg_proj_kernel(x_ref, nw_ref, w_ref, *refs):
    n_cast = (len(refs) - 4) // 2
    cast_in, (q_ref, i_ref, ff_ref, fb_ref), cast_out = refs[:n_cast], refs[n_cast:n_cast + 4], refs[n_cast + 4:]

    def stages(rows):
        h = _rms(x_ref[rows, :], nw_ref[...]).astype(BF16)
        yield
        for n, ref in enumerate((q_ref, i_ref, ff_ref, fb_ref)):
            ref[rows, :] = _dot(h, w_ref[:, n * HG_WIDTH:(n + 1) * HG_WIDTH]).astype(ref.dtype)
            yield

    _run_skewed(stages(pl.ds(s * SUB_PROJ, SUB_PROJ)) for s in range(TM_PROJ // SUB_PROJ))
    for src, dst in zip(cast_in, cast_out):
        dst[...] = src[...].astype(dst.dtype)


def _row_slab_spec(shape, n_steps):
    rows, cols = shape
    n_slabs = max(n for n in range(1, n_steps + 1)
                  if rows % n == 0 and (rows // n) % BF16_SUBLANES == 0)
    return pl.BlockSpec((rows // n_slabs, cols), lambda i: (jnp.minimum(i, n_slabs - 1), 0))


def _hg_proj(x2, pre_w, w_hg, later_weights):
    n_tok = x2.shape[0]
    n_steps = n_tok // TM_PROJ
    half = jax.ShapeDtypeStruct((n_tok, HG_WIDTH), BF16)
    full = jax.ShapeDtypeStruct((n_tok, HG_WIDTH), F32)
    tile = pl.BlockSpec((TM_PROJ, HG_WIDTH), lambda i: (i, 0))
    slabs = [_row_slab_spec(w.shape, n_steps) for w in later_weights]
    return pl.pallas_call(
        _hg_proj_kernel,
        grid=(n_steps,),
        in_specs=[pl.BlockSpec((TM_PROJ, D_MODEL), lambda i: (i, 0)),
                  _const_spec((1, D_MODEL)),
                  _const_spec((D_MODEL, HG_COLS))] + slabs,
        out_specs=[tile] * 4 + slabs,
        out_shape=[half, half, full, full] + [jax.ShapeDtypeStruct(w.shape, BF16)
                                              for w in later_weights],
        compiler_params=pltpu.CompilerParams(
            dimension_semantics=("arbitrary",), vmem_limit_bytes=VMEM_LIMIT_BYTES),
        name="hg_in_proj",
    )(x2, pre_w, w_hg, *later_weights)


class _ScanScratch(NamedTuple):
    st: Any
    k: Any
    b: Any
    kd: Any
    qe: Any
    dec: Any
    sc: Any


def _scan_scratch_shapes():
    stage = (2, N_CHUNKS, HG_CHUNK, HG_WIDTH)
    slots = (2,) + stage
    return list(_ScanScratch(
        st=pltpu.VMEM((2, HG_HEADS, HG_HEAD_DIM, HG_HEAD_DIM), F32),
        k=pltpu.VMEM(stage, F32),
        b=pltpu.VMEM(stage, F32),
        kd=pltpu.VMEM(slots, BF16), qe=pltpu.VMEM(slots, BF16),
        dec=pltpu.VMEM((2, 2, N_CHUNKS, 1, HG_WIDTH), F32),
        sc=pltpu.VMEM((2, 2, N_CHUNKS, HG_HEADS, HG_CHUNK, HG_CHUNK), BF16)))


_UNITS = [(d, c) for c in range(N_CHUNKS) for d in range(2)]
_HEADS = [slice(h * HG_HEAD_DIM, (h + 1) * HG_HEAD_DIM) for h in range(HG_HEADS)]


def _chunk_rows(c):
    return pl.ds(c * HG_CHUNK, HG_CHUNK)


def _scan_prepare(slot, lb, masks, tris, q_refs, f_refs, s):
    mid = HG_CHUNK // 2
    ref_rows, last_rows = (mid - 1, mid), (HG_CHUNK - 1, 0)

    def gates(d, c):
        lower = lb[d:d + 1, :]
        f = lower + (1.0 - lower) * jax.nn.sigmoid(f_refs[d][_chunk_rows(c), :])
        s.k[d, c] = 1.0 - f
        lf = jnp.log(f)
        hi = lf.astype(BF16)
        lo = (lf - hi.astype(F32)).astype(BF16)
        s.b[d, c] = _dot(tris[d], jnp.concatenate([hi, lo], axis=0))

    def decays(d, c):
        b = s.b[d, c]
        b_mid = b[ref_rows[d]:ref_rows[d] + 1, :]
        b_last = b[last_rows[d]:last_rows[d] + 1, :]
        t = (b - b_mid) * LOG2_E
        qr = q_refs[d][_chunk_rows(c), :].astype(F32) * jnp.exp2(t)
        kr = s.k[d, c] * jnp.exp2(-t)
        s.qe[slot, d, c] = (qr * jnp.exp(b_mid)).astype(BF16)
        s.kd[slot, d, c] = (kr * jnp.exp(b_last - b_mid)).astype(BF16)
        s.dec[slot, d, c] = jnp.exp(b_last)
        qr = qr.astype(BF16)
        kr = kr.astype(BF16)
        for h, sl in enumerate(_HEADS):
            sc = lax.dot_general(qr[:, sl], kr[:, sl], _NT, preferred_element_type=F32)
            s.sc[slot, d, c, h] = jnp.where(masks[d], sc, 0.0).astype(BF16)

    return [functools.partial(phase, d, c) for phase in (gates, decays) for d, c in _UNITS]


def _scan_apply(slot, v_refs, o_refs, reset, s):
    def reset_state():
        s.st[...] = jnp.where(reset, 0.0, s.st[...])

    def output_and_state(d, c):
        decay = s.dec[slot, d, c]
        vb = v_refs[d][_chunk_rows(c), :].astype(BF16)
        for h, sl in enumerate(_HEADS):
            st = s.st[d, h]
            o_refs[d][_chunk_rows(c), sl] = _dot(s.sc[slot, d, c, h], vb[:, sl]) + lax.dot_general(
                s.qe[slot, d, c, :, sl], st.astype(BF16), _NT, preferred_element_type=F32)
            s.st[d, h] = st * decay[:, sl] + lax.dot_general(
                vb[:, sl], s.kd[slot, d, c, :, sl], _TN, preferred_element_type=F32)

    thunks = [] if reset is None else [reset_state]
    for step in range(N_CHUNKS):
        thunks += [functools.partial(output_and_state, 0, step),
                   functools.partial(output_and_state, 1, N_CHUNKS - 1 - step)]
    return thunks


def _scan_kernel(lbl_ref, q0f, f0f, q0b, f0b, qaf, faf, qbf, fbf, qab, fab, qbb, fbb,
                 vf_ref, vb_ref, of_ref, ob_ref, *scratch, layer, steps_per_row):
    s = _ScanScratch(*scratch)
    u = pl.program_id(0)
    lb = _lower_bounds(lbl_ref, layer)
    row = lax.broadcasted_iota(jnp.int32, (HG_CHUNK, HG_CHUNK), 0)
    col = lax.broadcasted_iota(jnp.int32, (HG_CHUNK, HG_CHUNK), 1)
    masks = (col <= row, col >= row)
    row2 = lax.broadcasted_iota(jnp.int32, (HG_CHUNK, 2 * HG_CHUNK), 0)
    col2 = lax.broadcasted_iota(jnp.int32, (HG_CHUNK, 2 * HG_CHUNK), 1) & (HG_CHUNK - 1)
    tris = tuple(jnp.where(m, 1.0, 0.0).astype(BF16) for m in (col2 <= row2, col2 >= row2))
    first, second = pl.ds(0, TB_SCAN), pl.ds(TB_SCAN, TB_SCAN)

    @pl.when(u == 0)
    def _():
        for thunk in _scan_prepare(0, lb, masks, tris, (q0f, q0b), (f0f, f0b), s):
            thunk()

    _interleave(
        _scan_prepare(1, lb, masks, tris, (qaf, qab), (faf, fab), s),
        _scan_apply(0, (vf_ref.at[first, :], vb_ref.at[second, :]),
                    (of_ref.at[first, :], ob_ref.at[second, :]), u % steps_per_row == 0, s))
    _interleave(
        _scan_prepare(0, lb, masks, tris, (qbf, qbb), (fbf, fbb), s),
        _scan_apply(1, (vf_ref.at[second, :], vb_ref.at[first, :]),
                    (of_ref.at[second, :], ob_ref.at[first, :]), None, s))


def _lower_bounds(lbl_ref, layer):
    logits = lbl_ref[...]
    e = jnp.exp(logits - jnp.max(logits, axis=0, keepdims=True))
    return jnp.sum(e[:layer + 1], axis=0) / jnp.sum(e, axis=0)


def _hg_scan(lb_logits, q, i, f_fw, f_bw, layer):
    bsz, seq, _ = q.shape
    nb = seq // TB_SCAN
    spr = nb // 2

    def blk(index_map, **kw):
        return pl.BlockSpec((None, TB_SCAN, HG_WIDTH), index_map, **kw)

    def pair(index_map):
        return pl.BlockSpec((None, 2 * TB_SCAN, HG_WIDTH), index_map)

    def next_row(u):
        return jnp.minimum(u // spr + 1, bsz - 1)

    def last_in_row(u):
        return u % spr == spr - 1

    once = dict(pipeline_mode=pl.Buffered(1))
    first_f = blk(lambda u: (0, 0, 0), **once)
    first_b = blk(lambda u: (0, nb - 1, 0), **once)
    second_f = blk(lambda u: (u // spr, 2 * (u % spr) + 1, 0))
    second_b = blk(lambda u: (u // spr, nb - 2 - 2 * (u % spr), 0))
    next_f = blk(lambda u: (jnp.where(last_in_row(u), next_row(u), u // spr),
                            jnp.where(last_in_row(u), 0, 2 * (u % spr) + 2), 0))
    next_b = blk(lambda u: (jnp.where(last_in_row(u), next_row(u), u // spr),
                            jnp.where(last_in_row(u), nb - 1, nb - 3 - 2 * (u % spr)), 0))
    pair_f = pair(lambda u: (u // spr, u % spr, 0))
    pair_b = pair(lambda u: (u // spr, spr - 1 - u % spr, 0))
    out = jax.ShapeDtypeStruct((bsz, seq, HG_WIDTH), F32)
    return pl.pallas_call(
        functools.partial(_scan_kernel, layer=layer, steps_per_row=spr),
        grid=(bsz * spr,),
        in_specs=[_const_spec(lb_logits.shape),
                  first_f, first_f, first_b, first_b,
                  second_f, second_f, next_f, next_f,
                  second_b, second_b, next_b, next_b,
                  pair_f, pair_b],
        out_specs=[pair_f, pair_b],
        out_shape=[out, out],
        scratch_shapes=_scan_scratch_shapes(),
        compiler_params=pltpu.CompilerParams(
            dimension_semantics=("arbitrary",), vmem_limit_bytes=VMEM_LIMIT_BYTES),
        name="hg_scan",
    )(lb_logits, q, f_fw, q, f_bw, q, f_fw, q, f_fw, q, f_bw, q, f_bw, i, i)


def _mix_stages(rows, x_ref, of_ref, ob_ref, prew_ref, win_ref, hgw_ref, lnw_ref, lnb_ref,
                ws_ref, bs_ref, wa_ref, wb_ref, wo_ref, postw_ref, x1_ref):
    x = x_ref[rows, :]
    h = _rms(x, prew_ref[...]).astype(BF16)

    def proj(lo, width):
        return _dot(h, win_ref[:, HG_COLS + lo:HG_COLS + lo + width])
    yield

    g = proj(0, HG_WIDTH)
    u = proj(HG_WIDTH, SG_WIDTH)
    v = proj(HG_WIDTH + SG_WIDTH, SG_WIDTH)
    yield

    o = of_ref[rows, :] + ob_ref[rows, :]
    heads = []
    for hd in range(HG_HEADS):
        oh = o[:, hd * HG_HEAD_DIM:(hd + 1) * HG_HEAD_DIM]
        heads.append(oh * lax.rsqrt(jnp.mean(oh * oh, axis=-1, keepdims=True) + EPS))
    o = jnp.concatenate(heads, axis=-1) * hgw_ref[...]
    a_in = (o * jax.nn.silu(g)).astype(BF16)
    u = jax.nn.gelu(u)
    v = jax.nn.gelu(v)
    mu = jnp.mean(v, axis=-1, keepdims=True)
    vc = v - mu
    v = vc * lax.rsqrt(jnp.mean(vc * vc, axis=-1, keepdims=True) + EPS)
    v = (v * lnw_ref[...] + lnb_ref[...]).astype(BF16)
    yield

    ga = proj(HG_WIDTH + 2 * SG_WIDTH, D_MODEL)
    y_a = _dot(a_in, wa_ref[...])
    chunks = []
    for c in range(SUB_MIX // SG_CHUNK):
        r = slice(c * SG_CHUNK, (c + 1) * SG_CHUNK)
        groups = []
        for gi in range(SG_GROUPS):
            sl = slice(gi * SG_GROUP_DIM, (gi + 1) * SG_GROUP_DIM)
            groups.append(_dot(ws_ref[gi], v[r, sl]) + bs_ref[:, gi:gi + 1])
        chunks.append(jnp.concatenate(groups, axis=-1))
    yield

    s_in = (u * jnp.concatenate(chunks, axis=0)).astype(BF16)
    merged_a = jax.nn.sigmoid(ga) * y_a
    yield

    gb = proj(HG_WIDTH + 2 * SG_WIDTH + D_MODEL, D_MODEL)
    y_b = _dot(s_in, wb_ref[...])
    yield

    merged = (merged_a + jax.nn.sigmoid(gb) * y_b).astype(BF16)
    yield

    mix = _dot(merged, wo_ref[...])
    yield

    x1_ref[rows, :] = x + _rms(mix, postw_ref[...])


def _mix_kernel(*refs):
    _run_skewed(_mix_stages(pl.ds(s * SUB_MIX, SUB_MIX), *refs)
                for s in range(TM_MIX // SUB_MIX))


def _mix(x2, o_f, o_b, pre_w, w_mix, hg_w, ln_w, ln_b, w_s, b_s_t, w_a, w_b, w_o, post_w):
    n_tok = x2.shape[0]
    return pl.pallas_call(
        _mix_kernel,
        grid=(n_tok // TM_MIX,),
        in_specs=[pl.BlockSpec((TM_MIX, D_MODEL), lambda i: (i, 0)),
                  pl.BlockSpec((TM_MIX, HG_WIDTH), lambda i: (i, 0)),
                  pl.BlockSpec((TM_MIX, HG_WIDTH), lambda i: (i, 0)),
                  _const_spec((1, D_MODEL)),
                  _const_spec((D_MODEL, HG_COLS + MIX_COLS)),
                  _const_spec((1, HG_WIDTH)),
                  _const_spec((1, SG_WIDTH)),
                  _const_spec((1, SG_WIDTH)),
                  _const_spec((SG_GROUPS, SG_CHUNK, SG_CHUNK)),
                  _const_spec((SG_CHUNK, SG_GROUPS)),
                  _const_spec((HG_WIDTH, D_MODEL)),
                  _const_spec((SG_WIDTH, D_MODEL)),
                  _const_spec((D_MODEL, D_MODEL)),
                  _const_spec((1, D_MODEL))],
        out_specs=pl.BlockSpec((TM_MIX, D_MODEL), lambda i: (i, 0)),
        out_shape=jax.ShapeDtypeStruct((n_tok, D_MODEL), F32),
        compiler_params=pltpu.CompilerParams(
            dimension_semantics=("parallel",), vmem_limit_bytes=VMEM_LIMIT_BYTES),
        name="mixer_tail",
    )(x2, o_f, o_b, pre_w, w_mix, hg_w, ln_w, ln_b, w_s, b_s_t, w_a, w_b, w_o, post_w)


def _ffn_kernel(x_ref, prew_ref, wg_ref, wu_ref, wd_ref, postw_ref, out_ref):
    def stages(rows):
        x = x_ref[rows, :]
        h = _rms(x, prew_ref[...]).astype(BF16)
        acc = jnp.zeros((SUB_FFN, D_MODEL), F32)
        yield
        for t in range(D_FF // FF_TILE):
            cols = slice(t * FF_TILE, (t + 1) * FF_TILE)
            act = jax.nn.silu(_dot(h, wg_ref[:, cols])) * _dot(h, wu_ref[:, cols])
            acc = acc + _dot(act.astype(BF16), wd_ref[cols, :])
            yield
        out_ref[rows, :] = x + _rms(acc, postw_ref[...])

    _run_skewed(stages(pl.ds(s * SUB_FFN, SUB_FFN)) for s in range(TM_FFN // SUB_FFN))


def _ffn(x1, pre_w, w_g, w_u, w_d, post_w):
    n_tok = x1.shape[0]
    tile = pl.BlockSpec((TM_FFN, D_MODEL), lambda i: (i, 0))
    return pl.pallas_call(
        _ffn_kernel,
        grid=(n_tok // TM_FFN,),
        in_specs=[tile,
                  _const_spec((1, D_MODEL)),
                  _const_spec((D_MODEL, D_FF)),
                  _const_spec((D_MODEL, D_FF)),
                  _const_spec((D_FF, D_MODEL)),
                  _const_spec((1, D_MODEL))],
        out_specs=tile,
        out_shape=jax.ShapeDtypeStruct((n_tok, D_MODEL), F32),
        compiler_params=pltpu.CompilerParams(
            dimension_semantics=("parallel",), vmem_limit_bytes=VMEM_LIMIT_BYTES),
        name="swiglu_ffn",
    )(x1, pre_w, w_g, w_u, w_d, post_w)


def kernel(x, pre_mix_w, w_in, lb_logits, hg_norm_w, sg_ln_w, sg_ln_b, sg_spatial_w,
           sg_spatial_b, w_proj_a, w_proj_b, w_out, post_mix_w, pre_ffn_w, w_gate, w_up,
           w_down, post_ffn_w):
    bsz, seq, d = x.shape
    depth = pre_mix_w.shape[0]
    assert d == D_MODEL and seq % (2 * TB_SCAN) == 0 and (bsz * seq) % TM_FFN == 0
    x2 = x.reshape(bsz * seq, d)
    lb_logits = lb_logits.astype(F32)
    for l in range(depth):
        w_hg = w_in[l][:, :HG_COLS].astype(BF16)
        q, i, f_fw, f_bw, w_in_b, w_a, w_b, w_o, w_g, w_u, w_d = _hg_proj(
            x2, pre_mix_w[l][None], w_hg,
            [w_in[l], w_proj_a[l], w_proj_b[l], w_out[l], w_gate[l], w_up[l], w_down[l]])
        shp = (bsz, seq, HG_WIDTH)
        o_f, o_b = _hg_scan(lb_logits, q.reshape(shp), i.reshape(shp),
                            f_fw.reshape(shp), f_bw.reshape(shp), l)
        x1 = _mix(x2, o_f.reshape(-1, HG_WIDTH), o_b.reshape(-1, HG_WIDTH),
                  pre_mix_w[l][None], w_in_b, hg_norm_w[l][None],
                  sg_ln_w[l][None], sg_ln_b[l][None], sg_spatial_w[l].astype(BF16),
                  sg_spatial_b[l].T, w_a, w_b, w_o, post_mix_w[l][None])
        x2 = _ffn(x1, pre_ffn_w[l][None], w_g, w_u, w_d, post_ffn_w[l][None])
    return x2.reshape(bsz, seq, d)
```

```python
import functools
from typing import Any, NamedTuple

import jax
import jax.numpy as jnp
from jax import lax
from jax.experimental import pallas as pl
from jax.experimental.pallas import tpu as pltpu

D_MODEL = 1024
HG_HEADS = 4
HG_HEAD_DIM = 128
HG_WIDTH = HG_HEADS * HG_HEAD_DIM
HG_CHUNK = 64
SG_GROUPS = 4
SG_GROUP_DIM = 128
SG_WIDTH = SG_GROUPS * SG_GROUP_DIM
SG_CHUNK = 128
D_FF = 2816
EPS = 1e-6
LOG2_E = 1.4426950408889634

HG_COLS = 4 * HG_WIDTH
MIX_COLS = HG_WIDTH + 2 * SG_WIDTH + 2 * D_MODEL

VMEM_LIMIT_BYTES = 56 * 1024 * 1024

TM_PROJ, SUB_PROJ = 1024, 512
TB_SCAN = 512
N_CHUNKS = TB_SCAN // HG_CHUNK
TM_MIX, SUB_MIX = 1024, 256
TM_FFN, SUB_FFN = 1024, 512
FF_TILE = 256

F32 = jnp.float32
BF16 = jnp.bfloat16
BF16_SUBLANES = 16

_NT = (((1,), (1,)), ((), ()))
_TN = (((0,), (0,)), ((), ()))


def _dot(a, b):
    return jnp.dot(a, b, preferred_element_type=F32)


def _rms(x, w):
    return x * lax.rsqrt(jnp.mean(x * x, axis=-1, keepdims=True) + EPS) * w


def _run_skewed(stage_iters):
    pending, live = list(stage_iters), []
    while pending or live:
        if pending:
            live.append(pending.pop(0))
        for it in list(live):
            if next(it, StopIteration) is StopIteration:
                live.remove(it)


def _interleave(main, fill):
    k = 0
    for idx, thunk in enumerate(main):
        while k < len(fill) and k * len(main) <= idx * len(fill):
            fill[k]()
            k += 1
        thunk()
    for thunk in fill[k:]:
        thunk()


def _const_spec(shape):
    zeros = (0,) * len(shape)
    return pl.BlockSpec(shape, lambda *_: zeros, pipeline_mode=pl.Buffered(1))


def _hg_proj_kernel(x_ref, nw_ref, w_ref, *refs):
    n_cast = (len(refs) - 4) // 2
    cast_in, (q_ref, i_ref, ff_ref, fb_ref), cast_out = refs[:n_cast], refs[n_cast:n_cast + 4], refs[n_cast + 4:]

    def stages(rows):
        h = _rms(x_ref[rows, :], nw_ref[...]).astype(BF16)
        yield
        for n, ref in enumerate((q_ref, i_ref, ff_ref, fb_ref)):
            ref[rows, :] = _dot(h, w_ref[:, n * HG_WIDTH:(n + 1) * HG_WIDTH]).astype(ref.dtype)
            yield

    _run_skewed(stages(pl.ds(s * SUB_PROJ, SUB_PROJ)) for s in range(TM_PROJ // SUB_PROJ))
    for src, dst in zip(cast_in, cast_out):
        dst[...] = src[...].astype(dst.dtype)


def _row_slab_spec(shape, n_steps):
    rows, cols = shape
    n_slabs = max(n for n in range(1, n_steps + 1)
                  if rows % n == 0 and (rows // n) % BF16_SUBLANES == 0)
    return pl.BlockSpec((rows // n_slabs, cols), lambda i: (jnp.minimum(i, n_slabs - 1), 0))


def _hg_proj(x2, pre_w, w_hg, later_weights):
    n_tok = x2.shape[0]
    n_steps = n_tok // TM_PROJ
    half = jax.ShapeDtypeStruct((n_tok, HG_WIDTH), BF16)
    full = jax.ShapeDtypeStruct((n_tok, HG_WIDTH), F32)
    tile = pl.BlockSpec((TM_PROJ, HG_WIDTH), lambda i: (i, 0))
    slabs = [_row_slab_spec(w.shape, n_steps) for w in later_weights]
    return pl.pallas_call(
        _hg_proj_kernel,
        grid=(n_steps,),
        in_specs=[pl.BlockSpec((TM_PROJ, D_MODEL), lambda i: (i, 0)),
                  _const_spec((1, D_MODEL)),
                  _const_spec((D_MODEL, HG_COLS))] + slabs,
        out_specs=[tile] * 4 + slabs,
        out_shape=[half, half, full, full] + [jax.ShapeDtypeStruct(w.shape, BF16)
                                              for w in later_weights],
        compiler_params=pltpu.CompilerParams(
            dimension_semantics=("arbitrary",), vmem_limit_bytes=VMEM_LIMIT_BYTES),
        name="hg_in_proj",
    )(x2, pre_w, w_hg, *later_weights)


class _ScanScratch(NamedTuple):
    st: Any
    k: Any
    b: Any
    kd: Any
    qe: Any
    dec: Any
    sc: Any


def _scan_scratch_shapes():
    stage = (2, N_CHUNKS, HG_CHUNK, HG_WIDTH)
    slots = (2,) + stage
    return list(_ScanScratch(
        st=pltpu.VMEM((2, HG_HEADS, HG_HEAD_DIM, HG_HEAD_DIM), F32),
        k=pltpu.VMEM(stage, F32),
        b=pltpu.VMEM(stage, F32),
        kd=pltpu.VMEM(slots, BF16), qe=pltpu.VMEM(slots, BF16),
        dec=pltpu.VMEM((2, 2, N_CHUNKS, 1, HG_WIDTH), F32),
        sc=pltpu.VMEM((2, 2, N_CHUNKS, HG_HEADS, HG_CHUNK, HG_CHUNK), BF16)))


_UNITS = [(d, c) for c in range(N_CHUNKS) for d in range(2)]
_HEADS = [slice(h * HG_HEAD_DIM, (h + 1) * HG_HEAD_DIM) for h in range(HG_HEADS)]


def _chunk_rows(c):
    return pl.ds(c * HG_CHUNK, HG_CHUNK)


def _scan_prepare(slot, lb, masks, tris, q_refs, f_refs, s):
    mid = HG_CHUNK // 2
    ref_rows, last_rows = (mid - 1, mid), (HG_CHUNK - 1, 0)

    def gates(d, c):
        lower = lb[d:d + 1, :]
        f = lower + (1.0 - lower) * jax.nn.sigmoid(f_refs[d][_chunk_rows(c), :])
        s.k[d, c] = 1.0 - f
        lf = jnp.log(f)
        hi = lf.astype(BF16)
        lo = (lf - hi.astype(F32)).astype(BF16)
        s.b[d, c] = _dot(tris[d], jnp.concatenate([hi, lo], axis=0))

    def decays(d, c):
        b = s.b[d, c]
        b_mid = b[ref_rows[d]:ref_rows[d] + 1, :]
        b_last = b[last_rows[d]:last_rows[d] + 1, :]
        t = (b - b_mid) * LOG2_E
        qr = q_refs[d][_chunk_rows(c), :].astype(F32) * jnp.exp2(t)
        kr = s.k[d, c] * jnp.exp2(-t)
        s.qe[slot, d, c] = (qr * jnp.exp(b_mid)).astype(BF16)
        s.kd[slot, d, c] = (kr * jnp.exp(b_last - b_mid)).astype(BF16)
        s.dec[slot, d, c] = jnp.exp(b_last)
        qr = qr.astype(BF16)
        kr = kr.astype(BF16)
        for h, sl in enumerate(_HEADS):
            sc = lax.dot_general(qr[:, sl], kr[:, sl], _NT, preferred_element_type=F32)
            s.sc[slot, d, c, h] = jnp.where(masks[d], sc, 0.0).astype(BF16)

    return [functools.partial(phase, d, c) for phase in (gates, decays) for d, c in _UNITS]


def _scan_apply(slot, v_refs, o_refs, reset, s):
    def reset_state():
        s.st[...] = jnp.where(reset, 0.0, s.st[...])

    def output_and_state(d, c):
        decay = s.dec[slot, d, c]
        vb = v_refs[d][_chunk_rows(c), :].astype(BF16)
        for h, sl in enumerate(_HEADS):
            st = s.st[d, h]
            o_refs[d][_chunk_rows(c), sl] = _dot(s.sc[slot, d, c, h], vb[:, sl]) + lax.dot_general(
                s.qe[slot, d, c, :, sl], st.astype(BF16), _NT, preferred_element_type=F32)
            s.st[d, h] = st * decay[:, sl] + lax.dot_general(
                vb[:, sl], s.kd[slot, d, c, :, sl], _TN, preferred_element_type=F32)

    thunks = [] if reset is None else [reset_state]
    for step in range(N_CHUNKS):
        thunks += [functools.partial(output_and_state, 0, step),
                   functools.partial(output_and_state, 1, N_CHUNKS - 1 - step)]
    return thunks


def _scan_kernel(lbl_ref, q0f, f0f, q0b, f0b, qaf, faf, qbf, fbf, qab, fab, qbb, fbb,
                 vf_ref, vb_ref, *refs, layer, steps_per_row, n_cast):
    cast_in, (of_ref, ob_ref) = refs[:n_cast], refs[n_cast:n_cast + 2]
    cast_out, scratch = refs[n_cast + 2:2 * n_cast + 2], refs[2 * n_cast + 2:]
    for src, dst in zip(cast_in, cast_out):
        dst[...] = src[...].astype(dst.dtype)
    s = _ScanScratch(*scratch)
    u = pl.program_id(0)
    lb = _lower_bounds(lbl_ref, layer)
    row = lax.broadcasted_iota(jnp.int32, (HG_CHUNK, HG_CHUNK), 0)
    col = lax.broadcasted_iota(jnp.int32, (HG_CHUNK, HG_CHUNK), 1)
    masks = (col <= row, col >= row)
    row2 = lax.broadcasted_iota(jnp.int32, (HG_CHUNK, 2 * HG_CHUNK), 0)
    col2 = lax.broadcasted_iota(jnp.int32, (HG_CHUNK, 2 * HG_CHUNK), 1) & (HG_CHUNK - 1)
    tris = tuple(jnp.where(m, 1.0, 0.0).astype(BF16) for m in (col2 <= row2, col2 >= row2))
    first, second = pl.ds(0, TB_SCAN), pl.ds(TB_SCAN, TB_SCAN)

    @pl.when(u == 0)
    def _():
        for thunk in _scan_prepare(0, lb, masks, tris, (q0f, q0b), (f0f, f0b), s):
            thunk()

    _interleave(
        _scan_prepare(1, lb, masks, tris, (qaf, qab), (faf, fab), s),
        _scan_apply(0, (vf_ref.at[first, :], vb_ref.at[second, :]),
                    (of_ref.at[first, :], ob_ref.at[second, :]), u % steps_per_row == 0, s))
    _interleave(
        _scan_prepare(0, lb, masks, tris, (qbf, qbb), (fbf, fbb), s),
        _scan_apply(1, (vf_ref.at[second, :], vb_ref.at[first, :]),
                    (of_ref.at[second, :], ob_ref.at[first, :]), None, s))


def _lower_bounds(lbl_ref, layer):
    logits = lbl_ref[...]
    e = jnp.exp(logits - jnp.max(logits, axis=0, keepdims=True))
    return jnp.sum(e[:layer + 1], axis=0) / jnp.sum(e, axis=0)


def _hg_scan(lb_logits, q, i, f_fw, f_bw, layer, later_weights):
    bsz, seq, _ = q.shape
    nb = seq // TB_SCAN
    spr = nb // 2
    slabs = [_row_slab_spec(w.shape, bsz * spr) for w in later_weights]

    def blk(index_map, **kw):
        return pl.BlockSpec((None, TB_SCAN, HG_WIDTH), index_map, **kw)

    def pair(index_map):
        return pl.BlockSpec((None, 2 * TB_SCAN, HG_WIDTH), index_map)

    def next_row(u):
        return jnp.minimum(u // spr + 1, bsz - 1)

    def last_in_row(u):
        return u % spr == spr - 1

    once = dict(pipeline_mode=pl.Buffered(1))
    first_f = blk(lambda u: (0, 0, 0), **once)
    first_b = blk(lambda u: (0, nb - 1, 0), **once)
    second_f = blk(lambda u: (u // spr, 2 * (u % spr) + 1, 0))
    second_b = blk(lambda u: (u // spr, nb - 2 - 2 * (u % spr), 0))
    next_f = blk(lambda u: (jnp.where(last_in_row(u), next_row(u), u // spr),
                            jnp.where(last_in_row(u), 0, 2 * (u % spr) + 2), 0))
    next_b = blk(lambda u: (jnp.where(last_in_row(u), next_row(u), u // spr),
                            jnp.where(last_in_row(u), nb - 1, nb - 3 - 2 * (u % spr)), 0))
    pair_f = pair(lambda u: (u // spr, u % spr, 0))
    pair_b = pair(lambda u: (u // spr, spr - 1 - u % spr, 0))
    out = jax.ShapeDtypeStruct((bsz, seq, HG_WIDTH), F32)
    return pl.pallas_call(
        functools.partial(_scan_kernel, layer=layer, steps_per_row=spr,
                          n_cast=len(later_weights)),
        grid=(bsz * spr,),
        in_specs=[_const_spec(lb_logits.shape),
                  first_f, first_f, first_b, first_b,
                  second_f, second_f, next_f, next_f,
                  second_b, second_b, next_b, next_b,
                  pair_f, pair_b] + slabs,
        out_specs=[pair_f, pair_b] + slabs,
        out_shape=[out, out] + [jax.ShapeDtypeStruct(w.shape, BF16) for w in later_weights],
        scratch_shapes=_scan_scratch_shapes(),
        compiler_params=pltpu.CompilerParams(
            dimension_semantics=("arbitrary",), vmem_limit_bytes=VMEM_LIMIT_BYTES),
        name="hg_scan",
    )(lb_logits, q, f_fw, q, f_bw, q, f_fw, q, f_fw, q, f_bw, q, f_bw, i, i, *later_weights)


def _mix_stages(rows, x_ref, of_ref, ob_ref, prew_ref, win_ref, hgw_ref, lnw_ref, lnb_ref,
                ws_ref, bs_ref, wa_ref, wb_ref, wo_ref, postw_ref, x1_ref):
    h = _rms(x_ref[rows, :], prew_ref[...]).astype(BF16)

    def proj(lo, width):
        return _dot(h, win_ref[:, HG_COLS + lo:HG_COLS + lo + width])
    yield

    g = proj(0, HG_WIDTH)
    u = proj(HG_WIDTH, SG_WIDTH)
    v = proj(HG_WIDTH + SG_WIDTH, SG_WIDTH)
    yield

    o = of_ref[rows, :] + ob_ref[rows, :]
    heads = []
    for hd in range(HG_HEADS):
        oh = o[:, hd * HG_HEAD_DIM:(hd + 1) * HG_HEAD_DIM]
        heads.append(oh * lax.rsqrt(jnp.mean(oh * oh, axis=-1, keepdims=True) + EPS))
    o = jnp.concatenate(heads, axis=-1) * hgw_ref[...]
    a_in = (o * jax.nn.silu(g)).astype(BF16)
    u = jax.nn.gelu(u)
    v = jax.nn.gelu(v)
    mu = jnp.mean(v, axis=-1, keepdims=True)
    vc = v - mu
    v = vc * lax.rsqrt(jnp.mean(vc * vc, axis=-1, keepdims=True) + EPS)
    v = (v * lnw_ref[...] + lnb_ref[...]).astype(BF16)
    yield

    ga = proj(HG_WIDTH + 2 * SG_WIDTH, D_MODEL)
    y_a = _dot(a_in, wa_ref[...])
    chunks = []
    for c in range(SUB_MIX // SG_CHUNK):
        r = slice(c * SG_CHUNK, (c + 1) * SG_CHUNK)
        groups = []
        for gi in range(SG_GROUPS):
            sl = slice(gi * SG_GROUP_DIM, (gi + 1) * SG_GROUP_DIM)
            groups.append(_dot(ws_ref[gi], v[r, sl]) + bs_ref[:, gi:gi + 1])
        chunks.append(jnp.concatenate(groups, axis=-1))
    yield

    s_in = (u * jnp.concatenate(chunks, axis=0)).astype(BF16)
    merged_a = jax.nn.sigmoid(ga) * y_a
    yield

    gb = proj(HG_WIDTH + 2 * SG_WIDTH + D_MODEL, D_MODEL)
    y_b = _dot(s_in, wb_ref[...])
    yield

    merged = (merged_a + jax.nn.sigmoid(gb) * y_b).astype(BF16)
    yield

    mix = _dot(merged, wo_ref[...])
    yield

    x1_ref[rows, :] = x_ref[rows, :] + _rms(mix, postw_ref[...])


def _mix_kernel(*refs):
    _run_skewed(_mix_stages(pl.ds(s * SUB_MIX, SUB_MIX), *refs)
                for s in range(TM_MIX // SUB_MIX))


def _mix(x2, o_f, o_b, pre_w, w_mix, hg_w, ln_w, ln_b, w_s, b_s_t, w_a, w_b, w_o, post_w):
    n_tok = x2.shape[0]
    return pl.pallas_call(
        _mix_kernel,
        grid=(n_tok // TM_MIX,),
        in_specs=[pl.BlockSpec((TM_MIX, D_MODEL), lambda i: (i, 0)),
                  pl.BlockSpec((TM_MIX, HG_WIDTH), lambda i: (i, 0)),
                  pl.BlockSpec((TM_MIX, HG_WIDTH), lambda i: (i, 0)),
                  _const_spec((1, D_MODEL)),
                  _const_spec((D_MODEL, HG_COLS + MIX_COLS)),
                  _const_spec((1, HG_WIDTH)),
                  _const_spec((1, SG_WIDTH)),
                  _const_spec((1, SG_WIDTH)),
                  _const_spec((SG_GROUPS, SG_CHUNK, SG_CHUNK)),
                  _const_spec((SG_CHUNK, SG_GROUPS)),
                  _const_spec((HG_WIDTH, D_MODEL)),
                  _const_spec((SG_WIDTH, D_MODEL)),
                  _const_spec((D_MODEL, D_MODEL)),
                  _const_spec((1, D_MODEL))],
        out_specs=pl.BlockSpec((TM_MIX, D_MODEL), lambda i: (i, 0)),
        out_shape=jax.ShapeDtypeStruct((n_tok, D_MODEL), F32),
        compiler_params=pltpu.CompilerParams(
            dimension_semantics=("parallel",), vmem_limit_bytes=VMEM_LIMIT_BYTES),
        name="mixer_tail",
    )(x2, o_f, o_b, pre_w, w_mix, hg_w, ln_w, ln_b, w_s, b_s_t, w_a, w_b, w_o, post_w)


def _ffn_kernel(x_ref, prew_ref, wg_ref, wu_ref, wd_ref, postw_ref, out_ref):
    def stages(rows):
        h = _rms(x_ref[rows, :], prew_ref[...]).astype(BF16)
        acc = jnp.zeros((SUB_FFN, D_MODEL), F32)
        yield
        for t in range(D_FF // FF_TILE):
            cols = slice(t * FF_TILE, (t + 1) * FF_TILE)
            act = jax.nn.silu(_dot(h, wg_ref[:, cols])) * _dot(h, wu_ref[:, cols])
            acc = acc + _dot(act.astype(BF16), wd_ref[cols, :])
            yield
        out_ref[rows, :] = x_ref[rows, :] + _rms(acc, postw_ref[...])

    _run_skewed(stages(pl.ds(s * SUB_FFN, SUB_FFN)) for s in range(TM_FFN // SUB_FFN))


def _ffn(x1, pre_w, w_g, w_u, w_d, post_w):
    n_tok = x1.shape[0]
    tile = pl.BlockSpec((TM_FFN, D_MODEL), lambda i: (i, 0))
    return pl.pallas_call(
        _ffn_kernel,
        grid=(n_tok // TM_FFN,),
        in_specs=[tile,
                  _const_spec((1, D_MODEL)),
                  _const_spec((D_MODEL, D_FF)),
                  _const_spec((D_MODEL, D_FF)),
                  _const_spec((D_FF, D_MODEL)),
                  _const_spec((1, D_MODEL))],
        out_specs=tile,
        out_shape=jax.ShapeDtypeStruct((n_tok, D_MODEL), F32),
        compiler_params=pltpu.CompilerParams(
            dimension_semantics=("parallel",), vmem_limit_bytes=VMEM_LIMIT_BYTES),
        name="swiglu_ffn",
    )(x1, pre_w, w_g, w_u, w_d, post_w)


def kernel(x, pre_mix_w, w_in, lb_logits, hg_norm_w, sg_ln_w, sg_ln_b, sg_spatial_w,
           sg_spatial_b, w_proj_a, w_proj_b, w_out, post_mix_w, pre_ffn_w, w_gate, w_up,
           w_down, post_ffn_w):
    bsz, seq, d = x.shape
    depth = pre_mix_w.shape[0]
    assert d == D_MODEL and seq % (2 * TB_SCAN) == 0 and (bsz * seq) % TM_FFN == 0
    x2 = x.reshape(bsz * seq, d)
    lb_logits = lb_logits.astype(F32)
    for l in range(depth):
        w_hg = w_in[l][:, :HG_COLS].astype(BF16)
        q, i, f_fw, f_bw, w_in_b, w_a, w_b, w_o = _hg_proj(
            x2, pre_mix_w[l][None], w_hg, [w_in[l], w_proj_a[l], w_proj_b[l], w_out[l]])
        shp = (bsz, seq, HG_WIDTH)
        o_f, o_b, w_g, w_u, w_d = _hg_scan(
            lb_logits, q.reshape(shp), i.reshape(shp), f_fw.reshape(shp), f_bw.reshape(shp), l,
            [w_gate[l], w_up[l], w_down[l]])
        x1 = _mix(x2, o_f.reshape(-1, HG_WIDTH), o_b.reshape(-1, HG_WIDTH),
                  pre_mix_w[l][None], w_in_b, hg_norm_w[l][None],
                  sg_ln_w[l][None], sg_ln_b[l][None], sg_spatial_w[l].astype(BF16),
                  sg_spatial_b[l].T, w_a, w_b, w_o, post_mix_w[l][None])
        x2 = _ffn(x1, pre_ffn_w[l][None], w_g, w_u, w_d, post_ffn_w[l][None])
    return x2.reshape(bsz, seq, d)
```

```python
import functools
from typing import Any, NamedTuple

import jax
import jax.numpy as jnp
from jax import lax
from jax.experimental import pallas as pl
from jax.experimental.pallas import tpu as pltpu

D_MODEL = 1024
HG_HEADS = 4
HG_HEAD_DIM = 128
HG_WIDTH = HG_HEADS * HG_HEAD_DIM
HG_CHUNK = 64
SG_GROUPS = 4
SG_GROUP_DIM = 128
SG_WIDTH = SG_GROUPS * SG_GROUP_DIM
SG_CHUNK = 128
D_FF = 2816
EPS = 1e-6
LOG2_E = 1.4426950408889634

HG_COLS = 4 * HG_WIDTH
MIX_COLS = HG_WIDTH + 2 * SG_WIDTH + 2 * D_MODEL

VMEM_LIMIT_BYTES = 56 * 1024 * 1024

TM_PROJ, SUB_PROJ = 1024, 512
TB_SCAN = 512
N_CHUNKS = TB_SCAN // HG_CHUNK
TM_MIX, SUB_MIX = 1024, 256
TM_FFN, SUB_FFN = 1024, 512
FF_TILE = 256

F32 = jnp.float32
BF16 = jnp.bfloat16
BF16_SUBLANES = 16

_NT = (((1,), (1,)), ((), ()))
_TN = (((0,), (0,)), ((), ()))


def _dot(a, b):
    return jnp.dot(a, b, preferred_element_type=F32)


def _rms(x, w):
    return x * lax.rsqrt(jnp.mean(x * x, axis=-1, keepdims=True) + EPS) * w


def _run_skewed(stage_iters):
    pending, live = list(stage_iters), []
    while pending or live:
        if pending:
            live.append(pending.pop(0))
        for it in list(live):
            if next(it, StopIteration) is StopIteration:
                live.remove(it)


def _interleave(main, fill):
    k = 0
    for idx, thunk in enumerate(main):
        while k < len(fill) and k * len(main) <= idx * len(fill):
            fill[k]()
            k += 1
        thunk()
    for thunk in fill[k:]:
        thunk()


def _const_spec(shape):
    zeros = (0,) * len(shape)
    return pl.BlockSpec(shape, lambda *_: zeros, pipeline_mode=pl.Buffered(1))


def _hg_proj_kernel(x_ref, nw_ref, w_ref, *refs):
    *refs, wb_ref = refs
    n_cast = (len(refs) - 4) // 2
    cast_in, (q_ref, i_ref, ff_ref, fb_ref), cast_out = refs[:n_cast], refs[n_cast:n_cast + 4], refs[n_cast + 4:]

    @pl.when(pl.program_id(0) == 0)
    def _():
        wb_ref[...] = w_ref[...].astype(BF16)

    def stages(rows):
        h = _rms(x_ref[rows, :], nw_ref[...]).astype(BF16)
        yield
        for n, ref in enumerate((q_ref, i_ref, ff_ref, fb_ref)):
            ref[rows, :] = _dot(h, wb_ref[:, n * HG_WIDTH:(n + 1) * HG_WIDTH]).astype(ref.dtype)
            yield

    _run_skewed(stages(pl.ds(s * SUB_PROJ, SUB_PROJ)) for s in range(TM_PROJ // SUB_PROJ))
    for src, dst in zip(cast_in, cast_out):
        dst[...] = src[...].astype(dst.dtype)


def _row_slab_spec(shape, n_steps):
    rows, cols = shape
    n_slabs = max(n for n in range(1, n_steps + 1)
                  if rows % n == 0 and (rows // n) % BF16_SUBLANES == 0)
    return pl.BlockSpec((rows // n_slabs, cols), lambda i: (jnp.minimum(i, n_slabs - 1), 0))


def _hg_proj(x2, pre_w, w_in, later_weights):
    n_tok = x2.shape[0]
    n_steps = n_tok // TM_PROJ
    half = jax.ShapeDtypeStruct((n_tok, HG_WIDTH), BF16)
    full = jax.ShapeDtypeStruct((n_tok, HG_WIDTH), F32)
    tile = pl.BlockSpec((TM_PROJ, HG_WIDTH), lambda i: (i, 0))
    slabs = [_row_slab_spec(w.shape, n_steps) for w in later_weights]
    return pl.pallas_call(
        _hg_proj_kernel,
        grid=(n_steps,),
        in_specs=[pl.BlockSpec((TM_PROJ, D_MODEL), lambda i: (i, 0)),
                  _const_spec((1, D_MODEL)),
                  _const_spec((D_MODEL, HG_COLS))] + slabs,
        out_specs=[tile] * 4 + slabs,
        out_shape=[half, half, full, full] + [jax.ShapeDtypeStruct(w.shape, BF16)
                                              for w in later_weights],
        scratch_shapes=[pltpu.VMEM((D_MODEL, HG_COLS), BF16)],
        compiler_params=pltpu.CompilerParams(
            dimension_semantics=("arbitrary",), vmem_limit_bytes=VMEM_LIMIT_BYTES),
        name="hg_in_proj",
    )(x2, pre_w, w_in, *later_weights)


class _ScanScratch(NamedTuple):
    st: Any
    k: Any
    b: Any
    kd: Any
    qe: Any
    dec: Any
    sc: Any


def _scan_scratch_shapes():
    stage = (2, N_CHUNKS, HG_CHUNK, HG_WIDTH)
    slots = (2,) + stage
    return list(_ScanScratch(
        st=pltpu.VMEM((2, HG_HEADS, HG_HEAD_DIM, HG_HEAD_DIM), F32),
        k=pltpu.VMEM(stage, F32),
        b=pltpu.VMEM(stage, F32),
        kd=pltpu.VMEM(slots, BF16), qe=pltpu.VMEM(slots, BF16),
        dec=pltpu.VMEM((2, 2, N_CHUNKS, 1, HG_WIDTH), F32),
        sc=pltpu.VMEM((2, 2, N_CHUNKS, HG_HEADS, HG_CHUNK, HG_CHUNK), BF16)))


_UNITS = [(d, c) for c in range(N_CHUNKS) for d in range(2)]
_HEADS = [slice(h * HG_HEAD_DIM, (h + 1) * HG_HEAD_DIM) for h in range(HG_HEADS)]


def _chunk_rows(c):
    return pl.ds(c * HG_CHUNK, HG_CHUNK)


def _scan_prepare(slot, lb, masks, tris, q_refs, f_refs, s):
    mid = HG_CHUNK // 2
    ref_rows, last_rows = (mid - 1, mid), (HG_CHUNK - 1, 0)

    def gates(d, c):
        lower = lb[d:d + 1, :]
        f = lower + (1.0 - lower) * jax.nn.sigmoid(f_refs[d][_chunk_rows(c), :])
        s.k[d, c] = 1.0 - f
        lf = jnp.log(f)
        hi = lf.astype(BF16)
        lo = (lf - hi.astype(F32)).astype(BF16)
        s.b[d, c] = _dot(tris[d], jnp.concatenate([hi, lo], axis=0))

    def decays(d, c):
        b = s.b[d, c]
        b_mid = b[ref_rows[d]:ref_rows[d] + 1, :]
        b_last = b[last_rows[d]:last_rows[d] + 1, :]
        t = (b - b_mid) * LOG2_E
        qr = q_refs[d][_chunk_rows(c), :].astype(F32) * jnp.exp2(t)
        kr = s.k[d, c] * jnp.exp2(-t)
        s.qe[slot, d, c] = (qr * jnp.exp(b_mid)).astype(BF16)
        s.kd[slot, d, c] = (kr * jnp.exp(b_last - b_mid)).astype(BF16)
        s.dec[slot, d, c] = jnp.exp(b_last)
        qr = qr.astype(BF16)
        kr = kr.astype(BF16)
        for h, sl in enumerate(_HEADS):
            sc = lax.dot_general(qr[:, sl], kr[:, sl], _NT, preferred_element_type=F32)
            s.sc[slot, d, c, h] = jnp.where(masks[d], sc, 0.0).astype(BF16)

    return [functools.partial(phase, d, c) for phase in (gates, decays) for d, c in _UNITS]


def _scan_apply(slot, v_refs, o_refs, reset, s):
    def reset_state():
        s.st[...] = jnp.where(reset, 0.0, s.st[...])

    def output_and_state(d, c):
        decay = s.dec[slot, d, c]
        vb = v_refs[d][_chunk_rows(c), :].astype(BF16)
        for h, sl in enumerate(_HEADS):
            st = s.st[d, h]
            o_refs[d][_chunk_rows(c), sl] = _dot(s.sc[slot, d, c, h], vb[:, sl]) + lax.dot_general(
                s.qe[slot, d, c, :, sl], st.astype(BF16), _NT, preferred_element_type=F32)
            s.st[d, h] = st * decay[:, sl] + lax.dot_general(
                vb[:, sl], s.kd[slot, d, c, :, sl], _TN, preferred_element_type=F32)

    thunks = [] if reset is None else [reset_state]
    for step in range(N_CHUNKS):
        thunks += [functools.partial(output_and_state, 0, step),
                   functools.partial(output_and_state, 1, N_CHUNKS - 1 - step)]
    return thunks


def _scan_kernel(lbl_ref, q0f, f0f, q0b, f0b, qaf, faf, qbf, fbf, qab, fab, qbb, fbb,
                 vf_ref, vb_ref, *refs, layer, steps_per_row, n_cast):
    cast_in, (of_ref, ob_ref) = refs[:n_cast], refs[n_cast:n_cast + 2]
    cast_out, scratch = refs[n_cast + 2:2 * n_cast + 2], refs[2 * n_cast + 2:]
    for src, dst in zip(cast_in, cast_out):
        dst[...] = src[...].astype(dst.dtype)
    s = _ScanScratch(*scratch)
    u = pl.program_id(0)
    lb = _lower_bounds(lbl_ref, layer)
    row = lax.broadcasted_iota(jnp.int32, (HG_CHUNK, HG_CHUNK), 0)
    col = lax.broadcasted_iota(jnp.int32, (HG_CHUNK, HG_CHUNK), 1)
    masks = (col <= row, col >= row)
    row2 = lax.broadcasted_iota(jnp.int32, (HG_CHUNK, 2 * HG_CHUNK), 0)
    col2 = lax.broadcasted_iota(jnp.int32, (HG_CHUNK, 2 * HG_CHUNK), 1) & (HG_CHUNK - 1)
    tris = tuple(jnp.where(m, 1.0, 0.0).astype(BF16) for m in (col2 <= row2, col2 >= row2))
    first, second = pl.ds(0, TB_SCAN), pl.ds(TB_SCAN, TB_SCAN)

    @pl.when(u == 0)
    def _():
        for thunk in _scan_prepare(0, lb, masks, tris, (q0f, q0b), (f0f, f0b), s):
            thunk()

    _interleave(
        _scan_prepare(1, lb, masks, tris, (qaf, qab), (faf, fab), s),
        _scan_apply(0, (vf_ref.at[first, :], vb_ref.at[second, :]),
                    (of_ref.at[first, :], ob_ref.at[second, :]), u % steps_per_row == 0, s))
    _interleave(
        _scan_prepare(0, lb, masks, tris, (qbf, qbb), (fbf, fbb), s),
        _scan_apply(1, (vf_ref.at[second, :], vb_ref.at[first, :]),
                    (of_ref.at[second, :], ob_ref.at[first, :]), None, s))


def _lower_bounds(lbl_ref, layer):
    logits = lbl_ref[...]
    e = jnp.exp(logits - jnp.max(logits, axis=0, keepdims=True))
    return jnp.sum(e[:layer + 1], axis=0) / jnp.sum(e, axis=0)


def _hg_scan(lb_logits, q, i, f_fw, f_bw, layer, later_weights):
    bsz, seq, _ = q.shape
    nb = seq // TB_SCAN
    spr = nb // 2
    slabs = [_row_slab_spec(w.shape, bsz * spr) for w in later_weights]

    def blk(index_map, **kw):
        return pl.BlockSpec((None, TB_SCAN, HG_WIDTH), index_map, **kw)

    def pair(index_map):
        return pl.BlockSpec((None, 2 * TB_SCAN, HG_WIDTH), index_map)

    def next_row(u):
        return jnp.minimum(u // spr + 1, bsz - 1)

    def last_in_row(u):
        return u % spr == spr - 1

    once = dict(pipeline_mode=pl.Buffered(1))
    first_f = blk(lambda u: (0, 0, 0), **once)
    first_b = blk(lambda u: (0, nb - 1, 0), **once)
    second_f = blk(lambda u: (u // spr, 2 * (u % spr) + 1, 0))
    second_b = blk(lambda u: (u // spr, nb - 2 - 2 * (u % spr), 0))
    next_f = blk(lambda u: (jnp.where(last_in_row(u), next_row(u), u // spr),
                            jnp.where(last_in_row(u), 0, 2 * (u % spr) + 2), 0))
    next_b = blk(lambda u: (jnp.where(last_in_row(u), next_row(u), u // spr),
                            jnp.where(last_in_row(u), nb - 1, nb - 3 - 2 * (u % spr)), 0))
    pair_f = pair(lambda u: (u // spr, u % spr, 0))
    pair_b = pair(lambda u: (u // spr, spr - 1 - u % spr, 0))
    out = jax.ShapeDtypeStruct((bsz, seq, HG_WIDTH), F32)
    return pl.pallas_call(
        functools.partial(_scan_kernel, layer=layer, steps_per_row=spr,
                          n_cast=len(later_weights)),
        grid=(bsz * spr,),
        in_specs=[_const_spec(lb_logits.shape),
                  first_f, first_f, first_b, first_b,
                  second_f, second_f, next_f, next_f,
                  second_b, second_b, next_b, next_b,
                  pair_f, pair_b] + slabs,
        out_specs=[pair_f, pair_b] + slabs,
        out_shape=[out, out] + [jax.ShapeDtypeStruct(w.shape, BF16) for w in later_weights],
        scratch_shapes=_scan_scratch_shapes(),
        compiler_params=pltpu.CompilerParams(
            dimension_semantics=("arbitrary",), vmem_limit_bytes=VMEM_LIMIT_BYTES),
        name="hg_scan",
    )(lb_logits, q, f_fw, q, f_bw, q, f_fw, q, f_fw, q, f_bw, q, f_bw, i, i, *later_weights)


def _mix_stages(rows, x_ref, of_ref, ob_ref, prew_ref, win_ref, hgw_ref, lnw_ref, lnb_ref,
                ws_ref, bs_ref, wa_ref, wb_ref, wo_ref, postw_ref, x1_ref):
    h = _rms(x_ref[rows, :], prew_ref[...]).astype(BF16)

    def proj(lo, width):
        return _dot(h, win_ref[:, HG_COLS + lo:HG_COLS + lo + width])
    yield

    g = proj(0, HG_WIDTH)
    u = proj(HG_WIDTH, SG_WIDTH)
    v = proj(HG_WIDTH + SG_WIDTH, SG_WIDTH)
    yield

    o = of_ref[rows, :] + ob_ref[rows, :]
    heads = []
    for hd in range(HG_HEADS):
        oh = o[:, hd * HG_HEAD_DIM:(hd + 1) * HG_HEAD_DIM]
        heads.append(oh * lax.rsqrt(jnp.mean(oh * oh, axis=-1, keepdims=True) + EPS))
    o = jnp.concatenate(heads, axis=-1) * hgw_ref[...]
    a_in = (o * jax.nn.silu(g)).astype(BF16)
    u = jax.nn.gelu(u)
    v = jax.nn.gelu(v)
    mu = jnp.mean(v, axis=-1, keepdims=True)
    vc = v - mu
    v = vc * lax.rsqrt(jnp.mean(vc * vc, axis=-1, keepdims=True) + EPS)
    v = (v * lnw_ref[...] + lnb_ref[...]).astype(BF16)
    yield

    ga = proj(HG_WIDTH + 2 * SG_WIDTH, D_MODEL)
    y_a = _dot(a_in, wa_ref[...])
    chunks = []
    for c in range(SUB_MIX // SG_CHUNK):
        r = slice(c * SG_CHUNK, (c + 1) * SG_CHUNK)
        groups = []
        for gi in range(SG_GROUPS):
            sl = slice(gi * SG_GROUP_DIM, (gi + 1) * SG_GROUP_DIM)
            groups.append(_dot(ws_ref[gi], v[r, sl]) + bs_ref[:, gi:gi + 1])
        chunks.append(jnp.concatenate(groups, axis=-1))
    yield

    s_in = (u * jnp.concatenate(chunks, axis=0)).astype(BF16)
    merged_a = jax.nn.sigmoid(ga) * y_a
    yield

    gb = proj(HG_WIDTH + 2 * SG_WIDTH + D_MODEL, D_MODEL)
    y_b = _dot(s_in, wb_ref[...])
    yield

    merged = (merged_a + jax.nn.sigmoid(gb) * y_b).astype(BF16)
    yield

    mix = _dot(merged, wo_ref[...])
    yield

    x1_ref[rows, :] = x_ref[rows, :] + _rms(mix, postw_ref[...])


def _mix_kernel(*refs):
    _run_skewed(_mix_stages(pl.ds(s * SUB_MIX, SUB_MIX), *refs)
                for s in range(TM_MIX // SUB_MIX))


def _mix(x2, o_f, o_b, pre_w, w_mix, hg_w, ln_w, ln_b, w_s, b_s_t, w_a, w_b, w_o, post_w):
    n_tok = x2.shape[0]
    return pl.pallas_call(
        _mix_kernel,
        grid=(n_tok // TM_MIX,),
        in_specs=[pl.BlockSpec((TM_MIX, D_MODEL), lambda i: (i, 0)),
                  pl.BlockSpec((TM_MIX, HG_WIDTH), lambda i: (i, 0)),
                  pl.BlockSpec((TM_MIX, HG_WIDTH), lambda i: (i, 0)),
                  _const_spec((1, D_MODEL)),
                  _const_spec((D_MODEL, HG_COLS + MIX_COLS)),
                  _const_spec((1, HG_WIDTH)),
                  _const_spec((1, SG_WIDTH)),
                  _const_spec((1, SG_WIDTH)),
                  _const_spec((SG_GROUPS, SG_CHUNK, SG_CHUNK)),
                  _const_spec((SG_CHUNK, SG_GROUPS)),
                  _const_spec((HG_WIDTH, D_MODEL)),
                  _const_spec((SG_WIDTH, D_MODEL)),
                  _const_spec((D_MODEL, D_MODEL)),
                  _const_spec((1, D_MODEL))],
        out_specs=pl.BlockSpec((TM_MIX, D_MODEL), lambda i: (i, 0)),
        out_shape=jax.ShapeDtypeStruct((n_tok, D_MODEL), F32),
        compiler_params=pltpu.CompilerParams(
            dimension_semantics=("parallel",), vmem_limit_bytes=VMEM_LIMIT_BYTES),
        name="mixer_tail",
    )(x2, o_f, o_b, pre_w, w_mix, hg_w, ln_w, ln_b, w_s, b_s_t, w_a, w_b, w_o, post_w)


def _ffn_kernel(x_ref, prew_ref, wg_ref, wu_ref, wd_ref, postw_ref, out_ref):
    def stages(rows):
        h = _rms(x_ref[rows, :], prew_ref[...]).astype(BF16)
        acc = jnp.zeros((SUB_FFN, D_MODEL), F32)
        yield
        for t in range(D_FF // FF_TILE):
            cols = slice(t * FF_TILE, (t + 1) * FF_TILE)
            act = jax.nn.silu(_dot(h, wg_ref[:, cols])) * _dot(h, wu_ref[:, cols])
            acc = acc + _dot(act.astype(BF16), wd_ref[cols, :])
            yield
        out_ref[rows, :] = x_ref[rows, :] + _rms(acc, postw_ref[...])

    _run_skewed(stages(pl.ds(s * SUB_FFN, SUB_FFN)) for s in range(TM_FFN // SUB_FFN))


def _ffn(x1, pre_w, w_g, w_u, w_d, post_w):
    n_tok = x1.shape[0]
    tile = pl.BlockSpec((TM_FFN, D_MODEL), lambda i: (i, 0))
    return pl.pallas_call(
        _ffn_kernel,
        grid=(n_tok // TM_FFN,),
        in_specs=[tile,
                  _const_spec((1, D_MODEL)),
                  _const_spec((D_MODEL, D_FF)),
                  _const_spec((D_MODEL, D_FF)),
                  _const_spec((D_FF, D_MODEL)),
                  _const_spec((1, D_MODEL))],
        out_specs=tile,
        out_shape=jax.ShapeDtypeStruct((n_tok, D_MODEL), F32),
        compiler_params=pltpu.CompilerParams(
            dimension_semantics=("parallel",), vmem_limit_bytes=VMEM_LIMIT_BYTES),
        name="swiglu_ffn",
    )(x1, pre_w, w_g, w_u, w_d, post_w)


def kernel(x, pre_mix_w, w_in, lb_logits, hg_norm_w, sg_ln_w, sg_ln_b, sg_spatial_w,
           sg_spatial_b, w_proj_a, w_proj_b, w_out, post_mix_w, pre_ffn_w, w_gate, w_up,
           w_down, post_ffn_w):
    bsz, seq, d = x.shape
    depth = pre_mix_w.shape[0]
    assert d == D_MODEL and seq % (2 * TB_SCAN) == 0 and (bsz * seq) % TM_FFN == 0
    x2 = x.reshape(bsz * seq, d)
    lb_logits = lb_logits.astype(F32)
    for l in range(depth):
        q, i, f_fw, f_bw, w_in_b, w_a, w_b, w_o = _hg_proj(
            x2, pre_mix_w[l][None], w_in[l], [w_in[l], w_proj_a[l], w_proj_b[l], w_out[l]])
        shp = (bsz, seq, HG_WIDTH)
        o_f, o_b, w_g, w_u, w_d = _hg_scan(
            lb_logits, q.reshape(shp), i.reshape(shp), f_fw.reshape(shp), f_bw.reshape(shp), l,
            [w_gate[l], w_up[l], w_down[l]])
        x1 = _mix(x2, o_f.reshape(-1, HG_WIDTH), o_b.reshape(-1, HG_WIDTH),
                  pre_mix_w[l][None], w_in_b, hg_norm_w[l][None],
                  sg_ln_w[l][None], sg_ln_b[l][None], sg_spatial_w[l].astype(BF16),
                  sg_spatial_b[l].T, w_a, w_b, w_o, post_mix_w[l][None])
        x2 = _ffn(x1, pre_ffn_w[l][None], w_g, w_u, w_d, post_ffn_w[l][None])
    return x2.reshape(bsz, seq, d)
```

```python
import functools
from typing import Any, NamedTuple

import jax
import jax.numpy as jnp
from jax import lax
from jax.experimental import pallas as pl
from jax.experimental.pallas import tpu as pltpu

D_MODEL = 1024
HG_HEADS = 4
HG_HEAD_DIM = 128
HG_WIDTH = HG_HEADS * HG_HEAD_DIM
HG_CHUNK = 64
SG_GROUPS = 4
SG_GROUP_DIM = 128
SG_WIDTH = SG_GROUPS * SG_GROUP_DIM
SG_CHUNK = 128
D_FF = 2816
EPS = 1e-6
LOG2_E = 1.4426950408889634

HG_COLS = 4 * HG_WIDTH
MIX_COLS = HG_WIDTH + 2 * SG_WIDTH + 2 * D_MODEL

VMEM_LIMIT_BYTES = 56 * 1024 * 1024

TM_PROJ, SUB_PROJ = 1024, 512
TB_SCAN = 512
N_CHUNKS = TB_SCAN // HG_CHUNK
TM_MIX, SUB_MIX = 1024, 256
TM_FFN, SUB_FFN = 1024, 256
FF_TILE = 256

F32 = jnp.float32
BF16 = jnp.bfloat16
BF16_SUBLANES = 16

_NT = (((1,), (1,)), ((), ()))
_TN = (((0,), (0,)), ((), ()))


def _dot(a, b):
    return jnp.dot(a, b, preferred_element_type=F32)


def _rms(x, w):
    return x * lax.rsqrt(jnp.mean(x * x, axis=-1, keepdims=True) + EPS) * w


def _run_skewed(stage_iters):
    pending, live = list(stage_iters), []
    while pending or live:
        if pending:
            live.append(pending.pop(0))
        for it in list(live):
            if next(it, StopIteration) is StopIteration:
                live.remove(it)


def _interleave(main, fill):
    k = 0
    for idx, thunk in enumerate(main):
        while k < len(fill) and k * len(main) <= idx * len(fill):
            fill[k]()
            k += 1
        thunk()
    for thunk in fill[k:]:
        thunk()


def _const_spec(shape):
    zeros = (0,) * len(shape)
    return pl.BlockSpec(shape, lambda *_: zeros, pipeline_mode=pl.Buffered(1))


def _hg_proj_kernel(x_ref, nw_ref, w_ref, *refs):
    *refs, wb_ref = refs
    n_cast = (len(refs) - 4) // 2
    cast_in, (q_ref, i_ref, ff_ref, fb_ref), cast_out = refs[:n_cast], refs[n_cast:n_cast + 4], refs[n_cast + 4:]

    @pl.when(pl.program_id(0) == 0)
    def _():
        wb_ref[...] = w_ref[...].astype(BF16)

    def stages(rows):
        h = _rms(x_ref[rows, :], nw_ref[...]).astype(BF16)
        yield
        for n, ref in enumerate((q_ref, i_ref, ff_ref, fb_ref)):
            ref[rows, :] = _dot(h, wb_ref[:, n * HG_WIDTH:(n + 1) * HG_WIDTH]).astype(ref.dtype)
            yield

    _run_skewed(stages(pl.ds(s * SUB_PROJ, SUB_PROJ)) for s in range(TM_PROJ // SUB_PROJ))
    for src, dst in zip(cast_in, cast_out):
        dst[...] = src[...].astype(dst.dtype)


def _row_slab_spec(shape, n_steps):
    rows, cols = shape
    n_slabs = max(n for n in range(1, n_steps + 1)
                  if rows % n == 0 and (rows // n) % BF16_SUBLANES == 0)
    return pl.BlockSpec((rows // n_slabs, cols), lambda i: (jnp.minimum(i, n_slabs - 1), 0))


def _hg_proj(x2, pre_w, w_in, later_weights):
    n_tok = x2.shape[0]
    n_steps = n_tok // TM_PROJ
    half = jax.ShapeDtypeStruct((n_tok, HG_WIDTH), BF16)
    full = jax.ShapeDtypeStruct((n_tok, HG_WIDTH), F32)
    tile = pl.BlockSpec((TM_PROJ, HG_WIDTH), lambda i: (i, 0))
    slabs = [_row_slab_spec(w.shape, n_steps) for w in later_weights]
    return pl.pallas_call(
        _hg_proj_kernel,
        grid=(n_steps,),
        in_specs=[pl.BlockSpec((TM_PROJ, D_MODEL), lambda i: (i, 0)),
                  _const_spec((1, D_MODEL)),
                  _const_spec((D_MODEL, HG_COLS))] + slabs,
        out_specs=[tile] * 4 + slabs,
        out_shape=[half, half, full, full] + [jax.ShapeDtypeStruct(w.shape, BF16)
                                              for w in later_weights],
        scratch_shapes=[pltpu.VMEM((D_MODEL, HG_COLS), BF16)],
        compiler_params=pltpu.CompilerParams(
            dimension_semantics=("arbitrary",), vmem_limit_bytes=VMEM_LIMIT_BYTES),
        name="hg_in_proj",
    )(x2, pre_w, w_in, *later_weights)


class _ScanScratch(NamedTuple):
    st: Any
    k: Any
    b: Any
    kd: Any
    qe: Any
    dec: Any
    sc: Any


def _scan_scratch_shapes():
    stage = (2, N_CHUNKS, HG_CHUNK, HG_WIDTH)
    slots = (2,) + stage
    return list(_ScanScratch(
        st=pltpu.VMEM((2, HG_HEADS, HG_HEAD_DIM, HG_HEAD_DIM), F32),
        k=pltpu.VMEM(stage, F32),
        b=pltpu.VMEM(stage, F32),
        kd=pltpu.VMEM(slots, BF16), qe=pltpu.VMEM(slots, BF16),
        dec=pltpu.VMEM((2, 2, N_CHUNKS, 1, HG_WIDTH), F32),
        sc=pltpu.VMEM((2, 2, N_CHUNKS, HG_HEADS, HG_CHUNK, HG_CHUNK), BF16)))


_UNITS = [(d, c) for c in range(N_CHUNKS) for d in range(2)]
_HEADS = [slice(h * HG_HEAD_DIM, (h + 1) * HG_HEAD_DIM) for h in range(HG_HEADS)]


def _chunk_rows(c):
    return pl.ds(c * HG_CHUNK, HG_CHUNK)


def _scan_prepare(slot, lb, masks, tris, q_refs, f_refs, s):
    mid = HG_CHUNK // 2
    ref_rows, last_rows = (mid - 1, mid), (HG_CHUNK - 1, 0)

    def gates(d, c):
        lower = lb[d:d + 1, :]
        f = 0.5 * (1.0 + lower) + (0.5 * (1.0 - lower)) * jnp.tanh(0.5 * f_refs[d][_chunk_rows(c), :])
        s.k[d, c] = 1.0 - f
        lf = jnp.log(f)
        hi = lf.astype(BF16)
        lo = (lf - hi.astype(F32)).astype(BF16)
        s.b[d, c] = _dot(tris[d], jnp.concatenate([hi, lo], axis=0))

    def decays(d, c):
        b = s.b[d, c]
        b_mid = b[ref_rows[d]:ref_rows[d] + 1, :]
        b_last = b[last_rows[d]:last_rows[d] + 1, :]
        t = (b - b_mid) * LOG2_E
        qr = q_refs[d][_chunk_rows(c), :].astype(F32) * jnp.exp2(t)
        kr = s.k[d, c] * jnp.exp2(-t)
        s.qe[slot, d, c] = (qr * jnp.exp(b_mid)).astype(BF16)
        s.kd[slot, d, c] = (kr * jnp.exp(b_last - b_mid)).astype(BF16)
        s.dec[slot, d, c] = jnp.exp(b_last)
        qr = qr.astype(BF16)
        kr = kr.astype(BF16)
        for h, sl in enumerate(_HEADS):
            sc = lax.dot_general(qr[:, sl], kr[:, sl], _NT, preferred_element_type=F32)
            s.sc[slot, d, c, h] = jnp.where(masks[d], sc, 0.0).astype(BF16)

    return [functools.partial(phase, d, c) for phase in (gates, decays) for d, c in _UNITS]


def _scan_apply(slot, v_refs, o_refs, reset, s):
    def reset_state():
        s.st[...] = jnp.where(reset, 0.0, s.st[...])

    def output_and_state(d, c):
        decay = s.dec[slot, d, c]
        vb = v_refs[d][_chunk_rows(c), :].astype(BF16)
        for h, sl in enumerate(_HEADS):
            st = s.st[d, h]
            o_refs[d][_chunk_rows(c), sl] = _dot(s.sc[slot, d, c, h], vb[:, sl]) + lax.dot_general(
                s.qe[slot, d, c, :, sl], st.astype(BF16), _NT, preferred_element_type=F32)
            s.st[d, h] = st * decay[:, sl] + lax.dot_general(
                vb[:, sl], s.kd[slot, d, c, :, sl], _TN, preferred_element_type=F32)

    thunks = [] if reset is None else [reset_state]
    for step in range(N_CHUNKS):
        thunks += [functools.partial(output_and_state, 0, step),
                   functools.partial(output_and_state, 1, N_CHUNKS - 1 - step)]
    return thunks


def _scan_kernel(lbl_ref, q0f, f0f, q0b, f0b, qaf, faf, qbf, fbf, qab, fab, qbb, fbb,
                 vf_ref, vb_ref, *refs, layer, steps_per_row, n_cast):
    cast_in, (of_ref, ob_ref) = refs[:n_cast], refs[n_cast:n_cast + 2]
    cast_out, scratch = refs[n_cast + 2:2 * n_cast + 2], refs[2 * n_cast + 2:]
    for src, dst in zip(cast_in, cast_out):
        dst[...] = src[...].astype(dst.dtype)
    s = _ScanScratch(*scratch)
    u = pl.program_id(0)
    lb = _lower_bounds(lbl_ref, layer)
    row = lax.broadcasted_iota(jnp.int32, (HG_CHUNK, HG_CHUNK), 0)
    col = lax.broadcasted_iota(jnp.int32, (HG_CHUNK, HG_CHUNK), 1)
    masks = (col <= row, col >= row)
    row2 = lax.broadcasted_iota(jnp.int32, (HG_CHUNK, 2 * HG_CHUNK), 0)
    col2 = lax.broadcasted_iota(jnp.int32, (HG_CHUNK, 2 * HG_CHUNK), 1) & (HG_CHUNK - 1)
    tris = tuple(jnp.where(m, 1.0, 0.0).astype(BF16) for m in (col2 <= row2, col2 >= row2))
    first, second = pl.ds(0, TB_SCAN), pl.ds(TB_SCAN, TB_SCAN)

    @pl.when(u == 0)
    def _():
        for thunk in _scan_prepare(0, lb, masks, tris, (q0f, q0b), (f0f, f0b), s):
            thunk()

    _interleave(
        _scan_prepare(1, lb, masks, tris, (qaf, qab), (faf, fab), s),
        _scan_apply(0, (vf_ref.at[first, :], vb_ref.at[second, :]),
                    (of_ref.at[first, :], ob_ref.at[second, :]), u % steps_per_row == 0, s))
    _interleave(
        _scan_prepare(0, lb, masks, tris, (qbf, qbb), (fbf, fbb), s),
        _scan_apply(1, (vf_ref.at[second, :], vb_ref.at[first, :]),
                    (of_ref.at[second, :], ob_ref.at[first, :]), None, s))


def _lower_bounds(lbl_ref, layer):
    logits = lbl_ref[...]
    e = jnp.exp(logits - jnp.max(logits, axis=0, keepdims=True))
    return jnp.sum(e[:layer + 1], axis=0) / jnp.sum(e, axis=0)


def _hg_scan(lb_logits, q, i, f_fw, f_bw, layer, later_weights):
    bsz, seq, _ = q.shape
    nb = seq // TB_SCAN
    spr = nb // 2
    slabs = [_row_slab_spec(w.shape, bsz * spr) for w in later_weights]

    def blk(index_map, **kw):
        return pl.BlockSpec((None, TB_SCAN, HG_WIDTH), index_map, **kw)

    def pair(index_map):
        return pl.BlockSpec((None, 2 * TB_SCAN, HG_WIDTH), index_map)

    def next_row(u):
        return jnp.minimum(u // spr + 1, bsz - 1)

    def last_in_row(u):
        return u % spr == spr - 1

    once = dict(pipeline_mode=pl.Buffered(1))
    first_f = blk(lambda u: (0, 0, 0), **once)
    first_b = blk(lambda u: (0, nb - 1, 0), **once)
    second_f = blk(lambda u: (u // spr, 2 * (u % spr) + 1, 0))
    second_b = blk(lambda u: (u // spr, nb - 2 - 2 * (u % spr), 0))
    next_f = blk(lambda u: (jnp.where(last_in_row(u), next_row(u), u // spr),
                            jnp.where(last_in_row(u), 0, 2 * (u % spr) + 2), 0))
    next_b = blk(lambda u: (jnp.where(last_in_row(u), next_row(u), u // spr),
                            jnp.where(last_in_row(u), nb - 1, nb - 3 - 2 * (u % spr)), 0))
    pair_f = pair(lambda u: (u // spr, u % spr, 0))
    pair_b = pair(lambda u: (u // spr, spr - 1 - u % spr, 0))
    out = jax.ShapeDtypeStruct((bsz, seq, HG_WIDTH), F32)
    return pl.pallas_call(
        functools.partial(_scan_kernel, layer=layer, steps_per_row=spr,
                          n_cast=len(later_weights)),
        grid=(bsz * spr,),
        in_specs=[_const_spec(lb_logits.shape),
                  first_f, first_f, first_b, first_b,
                  second_f, second_f, next_f, next_f,
                  second_b, second_b, next_b, next_b,
                  pair_f, pair_b] + slabs,
        out_specs=[pair_f, pair_b] + slabs,
        out_shape=[out, out] + [jax.ShapeDtypeStruct(w.shape, BF16) for w in later_weights],
        scratch_shapes=_scan_scratch_shapes(),
        compiler_params=pltpu.CompilerParams(
            dimension_semantics=("arbitrary",), vmem_limit_bytes=VMEM_LIMIT_BYTES),
        name="hg_scan",
    )(lb_logits, q, f_fw, q, f_bw, q, f_fw, q, f_fw, q, f_bw, q, f_bw, i, i, *later_weights)


def _mix_stages(rows, x_ref, of_ref, ob_ref, prew_ref, win_ref, hgw_ref, lnw_ref, lnb_ref,
                ws_ref, bs_ref, wa_ref, wb_ref, wo_ref, postw_ref, x1_ref):
    h = _rms(x_ref[rows, :], prew_ref[...]).astype(BF16)

    def proj(lo, width):
        return _dot(h, win_ref[:, HG_COLS + lo:HG_COLS + lo + width])
    yield

    g = proj(0, HG_WIDTH)
    u = proj(HG_WIDTH, SG_WIDTH)
    v = proj(HG_WIDTH + SG_WIDTH, SG_WIDTH)
    yield

    o = of_ref[rows, :] + ob_ref[rows, :]
    heads = []
    for hd in range(HG_HEADS):
        oh = o[:, hd * HG_HEAD_DIM:(hd + 1) * HG_HEAD_DIM]
        heads.append(oh * lax.rsqrt(jnp.mean(oh * oh, axis=-1, keepdims=True) + EPS))
    o = jnp.concatenate(heads, axis=-1) * hgw_ref[...]
    a_in = (o * jax.nn.silu(g)).astype(BF16)
    u = jax.nn.gelu(u)
    v = jax.nn.gelu(v)
    mu = jnp.mean(v, axis=-1, keepdims=True)
    vc = v - mu
    v = vc * lax.rsqrt(jnp.mean(vc * vc, axis=-1, keepdims=True) + EPS)
    v = (v * lnw_ref[...] + lnb_ref[...]).astype(BF16)
    yield

    ga = proj(HG_WIDTH + 2 * SG_WIDTH, D_MODEL)
    y_a = _dot(a_in, wa_ref[...])
    chunks = []
    for c in range(SUB_MIX // SG_CHUNK):
        r = slice(c * SG_CHUNK, (c + 1) * SG_CHUNK)
        groups = []
        for gi in range(SG_GROUPS):
            sl = slice(gi * SG_GROUP_DIM, (gi + 1) * SG_GROUP_DIM)
            groups.append(_dot(ws_ref[gi], v[r, sl]) + bs_ref[:, gi:gi + 1])
        chunks.append(jnp.concatenate(groups, axis=-1))
    yield

    s_in = (u * jnp.concatenate(chunks, axis=0)).astype(BF16)
    merged_a = jax.nn.sigmoid(ga) * y_a
    yield

    gb = proj(HG_WIDTH + 2 * SG_WIDTH + D_MODEL, D_MODEL)
    y_b = _dot(s_in, wb_ref[...])
    yield

    merged = (merged_a + jax.nn.sigmoid(gb) * y_b).astype(BF16)
    yield

    mix = _dot(merged, wo_ref[...])
    yield

    x1_ref[rows, :] = x_ref[rows, :] + _rms(mix, postw_ref[...])


def _mix_kernel(*refs):
    _run_skewed(_mix_stages(pl.ds(s * SUB_MIX, SUB_MIX), *refs)
                for s in range(TM_MIX // SUB_MIX))


def _mix(x2, o_f, o_b, pre_w, w_mix, hg_w, ln_w, ln_b, w_s, b_s_t, w_a, w_b, w_o, post_w):
    n_tok = x2.shape[0]
    return pl.pallas_call(
        _mix_kernel,
        grid=(n_tok // TM_MIX,),
        in_specs=[pl.BlockSpec((TM_MIX, D_MODEL), lambda i: (i, 0)),
                  pl.BlockSpec((TM_MIX, HG_WIDTH), lambda i: (i, 0)),
                  pl.BlockSpec((TM_MIX, HG_WIDTH), lambda i: (i, 0)),
                  _const_spec((1, D_MODEL)),
                  _const_spec((D_MODEL, HG_COLS + MIX_COLS)),
                  _const_spec((1, HG_WIDTH)),
                  _const_spec((1, SG_WIDTH)),
                  _const_spec((1, SG_WIDTH)),
                  _const_spec((SG_GROUPS, SG_CHUNK, SG_CHUNK)),
                  _const_spec((SG_CHUNK, SG_GROUPS)),
                  _const_spec((HG_WIDTH, D_MODEL)),
                  _const_spec((SG_WIDTH, D_MODEL)),
                  _const_spec((D_MODEL, D_MODEL)),
                  _const_spec((1, D_MODEL))],
        out_specs=pl.BlockSpec((TM_MIX, D_MODEL), lambda i: (i, 0)),
        out_shape=jax.ShapeDtypeStruct((n_tok, D_MODEL), F32),
        compiler_params=pltpu.CompilerParams(
            dimension_semantics=("parallel",), vmem_limit_bytes=VMEM_LIMIT_BYTES),
        name="mixer_tail",
    )(x2, o_f, o_b, pre_w, w_mix, hg_w, ln_w, ln_b, w_s, b_s_t, w_a, w_b, w_o, post_w)


def _ffn_kernel(x_ref, prew_ref, wg_ref, wu_ref, wd_ref, postw_ref, out_ref):
    def stages(rows):
        h = _rms(x_ref[rows, :], prew_ref[...]).astype(BF16)
        yield
        acts = []
        for t in range(D_FF // FF_TILE):
            cols = slice(t * FF_TILE, (t + 1) * FF_TILE)
            act = jax.nn.silu(_dot(h, wg_ref[:, cols])) * _dot(h, wu_ref[:, cols])
            acts.append(act.astype(BF16))
            yield
        act = jnp.concatenate(acts, axis=1)
        outs = []
        for n in range(D_MODEL // FF_TILE):
            outs.append(_dot(act, wd_ref[:, n * FF_TILE:(n + 1) * FF_TILE]))
            yield
        ff = jnp.concatenate(outs, axis=1)
        out_ref[rows, :] = x_ref[rows, :] + _rms(ff, postw_ref[...])

    _run_skewed(stages(pl.ds(s * SUB_FFN, SUB_FFN)) for s in range(TM_FFN // SUB_FFN))


def _ffn(x1, pre_w, w_g, w_u, w_d, post_w):
    n_tok = x1.shape[0]
    tile = pl.BlockSpec((TM_FFN, D_MODEL), lambda i: (i, 0))
    return pl.pallas_call(
        _ffn_kernel,
        grid=(n_tok // TM_FFN,),
        in_specs=[tile,
                  _const_spec((1, D_MODEL)),
                  _const_spec((D_MODEL, D_FF)),
                  _const_spec((D_MODEL, D_FF)),
                  _const_spec((D_FF, D_MODEL)),
                  _const_spec((1, D_MODEL))],
        out_specs=tile,
        out_shape=jax.ShapeDtypeStruct((n_tok, D_MODEL), F32),
        compiler_params=pltpu.CompilerParams(
            dimension_semantics=("parallel",), vmem_limit_bytes=VMEM_LIMIT_BYTES),
        name="swiglu_ffn",
    )(x1, pre_w, w_g, w_u, w_d, post_w)


def kernel(x, pre_mix_w, w_in, lb_logits, hg_norm_w, sg_ln_w, sg_ln_b, sg_spatial_w,
           sg_spatial_b, w_proj_a, w_proj_b, w_out, post_mix_w, pre_ffn_w, w_gate, w_up,
           w_down, post_ffn_w):
    bsz, seq, d = x.shape
    depth = pre_mix_w.shape[0]
    assert d == D_MODEL and seq % (2 * TB_SCAN) == 0 and (bsz * seq) % TM_FFN == 0
    x2 = x.reshape(bsz * seq, d)
    lb_logits = lb_logits.astype(F32)
    for l in range(depth):
        q, i, f_fw, f_bw, w_in_b, w_a, w_b, w_o = _hg_proj(
            x2, pre_mix_w[l][None], w_in[l], [w_in[l], w_proj_a[l], w_proj_b[l], w_out[l]])
        shp = (bsz, seq, HG_WIDTH)
        o_f, o_b, w_g, w_u, w_d = _hg_scan(
            lb_logits, q.reshape(shp), i.reshape(shp), f_fw.reshape(shp), f_bw.reshape(shp), l,
            [w_gate[l], w_up[l], w_down[l]])
        x1 = _mix(x2, o_f.reshape(-1, HG_WIDTH), o_b.reshape(-1, HG_WIDTH),
                  pre_mix_w[l][None], w_in_b, hg_norm_w[l][None],
                  sg_ln_w[l][None], sg_ln_b[l][None], sg_spatial_w[l].astype(BF16),
                  sg_spatial_b[l].T, w_a, w_b, w_o, post_mix_w[l][None])
        x2 = _ffn(x1, pre_ffn_w[l][None], w_g, w_u, w_d, post_ffn_w[l][None])
    return x2.reshape(bsz, seq, d)
```
